```python
import math
import jax, jax.numpy as jnp
from jax import lax
import numpy as np

D_MODEL = 1024
BATCH = 16
SEQ = 2048
DEPTH = 2

S5_WIDTH = D_MODEL // 2
S5_GROUP = 16
S5_GROUPS = S5_WIDTH // S5_GROUP
S5_STATE = 64
DT_MIN = 1e-3
DT_MAX = 1e-1
SB_WIDTH = D_MODEL - S5_WIDTH
SB_HEAD_DIM = 64
SB_HEADS = SB_WIDTH // SB_HEAD_DIM
SB_BLOCK = 128
EVEN_IN = S5_WIDTH + 3 * SB_WIDTH
CONV_WIDTH = D_MODEL
CONV_SIZE = 31
FFN_HIDDEN = 4 * D_MODEL
N_EVEN = (DEPTH + 1) // 2
N_ODD = DEPTH // 2
DEEPNORM_ALPHA = (2 * DEPTH) ** 0.25
DEEPNORM_BETA = (8 * DEPTH) ** -0.25
LN_EPS = 1e-5

kernel_name = "s5_stickbreak_conformer_deepnorm_hybrid"


def layer_norm(x, g, b):
    xf = x.astype(jnp.float32)
    mu = jnp.mean(xf, axis=-1, keepdims=True)
    var = jnp.mean(jnp.square(xf - mu), axis=-1, keepdims=True)
    y = (xf - mu) * lax.rsqrt(var + LN_EPS)
    return (y * g.astype(jnp.float32) + b.astype(jnp.float32)).astype(x.dtype)


def s5_mixer(u, lam_re, lam_im, log_dt, b_re, b_im, c_re, c_im, d, w_glu, b_glu):
    bsz, seq, _ = u.shape
    f32 = jnp.float32
    lam = lax.complex(jnp.minimum(lam_re.astype(f32), -1e-4), lam_im.astype(f32))
    dt = jnp.exp(log_dt.astype(f32))[:, None]
    lam_bar = jnp.exp(lam * dt)
    b_bar = ((lam_bar - 1) / lam)[..., None] * lax.complex(b_re.astype(f32), b_im.astype(f32))
    uf = u.astype(f32)
    ug = uf.reshape(bsz, seq, S5_GROUPS, S5_GROUP).astype(jnp.complex64)
    bu = jnp.einsum('blgp,gnp->blgn', ug, b_bar)
    a = jnp.broadcast_to(lam_bar, bu.shape)

    def combine(left, right):
        a_l, b_l = left
        a_r, b_r = right
        return a_r * a_l, a_r * b_l + b_r

    _, states = lax.associative_scan(combine, (a, bu), axis=1)
    c = lax.complex(c_re.astype(f32), c_im.astype(f32))
    y = jnp.real(jnp.einsum('blgn,gpn->blgp', states, c)).reshape(bsz, seq, S5_WIDTH)
    y = jax.nn.gelu(y + d.astype(f32) * uf).astype(u.dtype)
    h = y @ w_glu + b_glu
    val, gate = jnp.split(h, 2, axis=-1)
    return val * jax.nn.sigmoid(gate)


def stick_breaking_attention(q, k, v):
    bsz, seq, nh, hd = q.shape
    n_blk = seq // SB_BLOCK
    scale = hd ** -0.5
    kf = k.astype(jnp.float32)
    vf = v.astype(jnp.float32)
    key_pos = jnp.arange(seq)
    qb = q.reshape(bsz, n_blk, SB_BLOCK, nh, hd).transpose(1, 0, 2, 3, 4)

    def one_block(args):
        q_blk, blk = args
        q_pos = blk * SB_BLOCK + jnp.arange(SB_BLOCK)
        z = jnp.einsum('bqhd,bkhd->bhqk', q_blk.astype(jnp.float32), kf) * scale
        strict = key_pos[None, :] < q_pos[:, None]
        log_beta = jax.nn.log_sigmoid(z)
        log_keep = jnp.where(strict, jax.nn.log_sigmoid(-z), 0.0)
        after = lax.cumsum(log_keep, axis=3, reverse=True) - log_keep
        weights = jnp.where(strict, jnp.exp(log_beta + after), 0.0)
        return jnp.einsum('bhqk,bkhd->bqhd', weights, vf)

    out = lax.map(one_block, (qb, jnp.arange(n_blk)))
    return out.transpose(1, 0, 2, 3, 4).reshape(bsz, seq, nh * hd).astype(q.dtype)


def conformer_conv(x, w_pw1, b_pw1, w_dw, b_dw, ln_g, ln_b, w_pw2, b_pw2):
    h = x @ w_pw1 + b_pw1
    val, gate = jnp.split(h, 2, axis=-1)
    h = val * jax.nn.sigmoid(gate)
    h = lax.conv_general_dilated(
        h, w_dw[:, None, :], window_strides=(1,), padding=[(CONV_SIZE - 1, 0)],
        dimension_numbers=('NWC', 'WIO', 'NWC'), feature_group_count=CONV_WIDTH) + b_dw
    h = jax.nn.silu(layer_norm(h, ln_g, ln_b))
    return h @ w_pw2 + b_pw2


def sqrelu_mlp(x, w1, b1, w2, b2):
    return jnp.square(jax.nn.relu(x @ w1 + b1)) @ w2 + b2


def setup_inputs(seed: int = 0) -> dict:
    key = jax.random.key(seed)
    keys = jax.random.split(key, 40)
    counter = [0]

    def nk():
        k = keys[counter[0]]
        counter[0] += 1
        return k

    def nrm(shape, scale):
        return scale * jax.random.normal(nk(), shape, jnp.float32)

    D = D_MODEL
    x = nrm((BATCH, SEQ, D), 1.0)
    ln1_g = 1.0 + nrm((DEPTH, D), 0.02)
    ln1_b = nrm((DEPTH, D), 0.01)
    ln2_g = 1.0 + nrm((DEPTH, D), 0.02)
    ln2_b = nrm((DEPTH, D), 0.01)
    ffn_w1 = nrm((DEPTH, D, FFN_HIDDEN), D ** -0.5)
    ffn_b1 = nrm((DEPTH, FFN_HIDDEN), 0.01)
    ffn_w2 = nrm((DEPTH, FFN_HIDDEN, D), FFN_HIDDEN ** -0.5 * DEEPNORM_BETA)
    ffn_b2 = nrm((DEPTH, D), 0.01)
    col_scale = jnp.concatenate([jnp.ones((S5_WIDTH + 2 * SB_WIDTH,), jnp.float32),
                                 jnp.full((SB_WIDTH,), DEEPNORM_BETA, jnp.float32)])
    mix_w_in = nrm((N_EVEN, D, EVEN_IN), D ** -0.5) * col_scale
    mix_b_in = nrm((N_EVEN, EVEN_IN), 0.01)
    s5_lambda_re = -0.5 + nrm((N_EVEN, S5_GROUPS, S5_STATE), 0.01)
    s5_lambda_im = (jnp.pi * jnp.arange(S5_STATE, dtype=jnp.float32)
                    + nrm((N_EVEN, S5_GROUPS, S5_STATE), 0.01))
    s5_log_dt = jax.random.uniform(nk(), (N_EVEN, S5_GROUPS), jnp.float32,
                                   minval=math.log(DT_MIN), maxval=math.log(DT_MAX))
    s5_b_re = nrm((N_EVEN, S5_GROUPS, S5_STATE, S5_GROUP), (2 * S5_GROUP) ** -0.5)
    s5_b_im = nrm((N_EVEN, S5_GROUPS, S5_STATE, S5_GROUP), (2 * S5_GROUP) ** -0.5)
    s5_c_re = nrm((N_EVEN, S5_GROUPS, S5_GROUP, S5_STATE), (2 * S5_STATE) ** -0.5)
    s5_c_im = nrm((N_EVEN, S5_GROUPS, S5_GROUP, S5_STATE), (2 * S5_STATE) ** -0.5)
    s5_d = nrm((N_EVEN, S5_WIDTH), 1.0)
    s5_w_glu = nrm((N_EVEN, S5_WIDTH, 2 * S5_WIDTH), S5_WIDTH ** -0.5)
    s5_b_glu = nrm((N_EVEN, 2 * S5_WIDTH), 0.01)
    mix_w_out = nrm((N_EVEN, S5_WIDTH + SB_WIDTH, D), (S5_WIDTH + SB_WIDTH) ** -0.5 * DEEPNORM_BETA)
    mix_b_out = nrm((N_EVEN, D), 0.01)
    conv_w_pw1 = nrm((N_ODD, D, 2 * CONV_WIDTH), D ** -0.5)
    conv_b_pw1 = nrm((N_ODD, 2 * CONV_WIDTH), 0.01)
    conv_w_dw = nrm((N_ODD, CONV_SIZE, CONV_WIDTH), CONV_SIZE ** -0.5)
    conv_b_dw = nrm((N_ODD, CONV_WIDTH), 0.01)
    conv_ln_g = 1.0 + nrm((N_ODD, CONV_WIDTH), 0.02)
    conv_ln_b = nrm((N_ODD, CONV_WIDTH), 0.01)
    conv_w_pw2 = nrm((N_ODD, CONV_WIDTH, D), CONV_WIDTH ** -0.5 * DEEPNORM_BETA)
    conv_b_pw2 = nrm((N_ODD, D), 0.01)
    return {
        "x": x, "ln1_g": ln1_g, "ln1_b": ln1_b, "ln2_g": ln2_g, "ln2_b": ln2_b,
        "ffn_w1": ffn_w1, "ffn_b1": ffn_b1, "ffn_w2": ffn_w2, "ffn_b2": ffn_b2,
        "mix_w_in": mix_w_in, "mix_b_in": mix_b_in,
        "s5_lambda_re": s5_lambda_re, "s5_lambda_im": s5_lambda_im, "s5_log_dt": s5_log_dt,
        "s5_b_re": s5_b_re, "s5_b_im": s5_b_im, "s5_c_re": s5_c_re, "s5_c_im": s5_c_im,
        "s5_d": s5_d, "s5_w_glu": s5_w_glu, "s5_b_glu": s5_b_glu,
        "mix_w_out": mix_w_out, "mix_b_out": mix_b_out,
        "conv_w_pw1": conv_w_pw1, "conv_b_pw1": conv_b_pw1, "conv_w_dw": conv_w_dw,
        "conv_b_dw": conv_b_dw, "conv_ln_g": conv_ln_g, "conv_ln_b": conv_ln_b,
        "conv_w_pw2": conv_w_pw2, "conv_b_pw2": conv_b_pw2,
    }


def reference(x, ln1_g, ln1_b, ln2_g, ln2_b, ffn_w1, ffn_b1, ffn_w2, ffn_b2,
              mix_w_in, mix_b_in, s5_lambda_re, s5_lambda_im, s5_log_dt,
              s5_b_re, s5_b_im, s5_c_re, s5_c_im, s5_d, s5_w_glu, s5_b_glu,
              mix_w_out, mix_b_out, conv_w_pw1, conv_b_pw1, conv_w_dw, conv_b_dw,
              conv_ln_g, conv_ln_b, conv_w_pw2, conv_b_pw2):
    bsz, seq, _ = x.shape
    for layer in range(DEPTH):
        i = layer // 2
        if layer % 2 == 0:
            h = x @ mix_w_in[i] + mix_b_in[i]
            u, q, k, v = jnp.split(
                h, [S5_WIDTH, S5_WIDTH + SB_WIDTH, S5_WIDTH + 2 * SB_WIDTH], axis=-1)
            s5_out = s5_mixer(u, s5_lambda_re[i], s5_lambda_im[i], s5_log_dt[i],
                              s5_b_re[i], s5_b_im[i], s5_c_re[i], s5_c_im[i],
                              s5_d[i], s5_w_glu[i], s5_b_glu[i])
            head_shape = (bsz, seq, SB_HEADS, SB_HEAD_DIM)
            sb_out = stick_breaking_attention(q.reshape(head_shape), k.reshape(head_shape),
                                              v.reshape(head_shape))
            mix = jnp.concatenate([s5_out, sb_out], axis=-1) @ mix_w_out[i] + mix_b_out[i]
        else:
            mix = conformer_conv(x, conv_w_pw1[i], conv_b_pw1[i], conv_w_dw[i], conv_b_dw[i],
                                 conv_ln_g[i], conv_ln_b[i], conv_w_pw2[i], conv_b_pw2[i])
        x = layer_norm(DEEPNORM_ALPHA * x + mix, ln1_g[layer], ln1_b[layer])
        ffn = sqrelu_mlp(x, ffn_w1[layer], ffn_b1[layer], ffn_w2[layer], ffn_b2[layer])
        x = layer_norm(DEEPNORM_ALPHA * x + ffn, ln2_g[layer], ln2_b[layer])
    return x
```

```python
import functools
import math

import jax
import jax.numpy as jnp
from jax import lax
from jax.experimental import pallas as pl
from jax.experimental.pallas import tpu as pltpu

F32 = jnp.float32
BF16 = jnp.bfloat16

LANES = 128
SUBLANES = 8
VMEM_LIMIT = 56 * 1024 * 1024

S5_GROUP = 16
S5_STATE = 64
SB_HEAD_DIM = 64
CONV_SIZE = 31
LN_EPS = 1e-5
DEPTH = 2
ALPHA = (2 * DEPTH) ** 0.25

ROW_TILE = 512
ATT_TILE = 256
S5_TIME = 64
CONV_TILE = 256
CONV_HIST = 32


def _params(*sem):
    return pltpu.CompilerParams(dimension_semantics=sem, vmem_limit_bytes=VMEM_LIMIT)


def _const_spec(shape):
    nd = len(shape)
    return pl.BlockSpec(shape, lambda *_: (0,) * nd, pipeline_mode=pl.Buffered(1))


def _layer_norm(r, g, b):
    mu = jnp.mean(r, axis=-1, keepdims=True)
    c = r - mu
    var = jnp.mean(c * c, axis=-1, keepdims=True)
    return c * lax.rsqrt(var + LN_EPS) * g + b


def _bdot(a, w):
    return jnp.dot(a.astype(BF16), w, preferred_element_type=F32)


def _in_proj_kernel(x_ref, w_ref, b_ref, u_ref, qkv_ref, *, s5w, sbw, scale):
    h = _bdot(x_ref[...], w_ref[...]) + b_ref[...]
    u_ref[...] = h[:, :s5w]
    qkv_ref[:, :sbw] = (h[:, s5w:s5w + sbw] * scale).astype(BF16)
    qkv_ref[:, sbw:] = h[:, s5w + sbw:].astype(BF16)


def _in_proj(x2, w, b, s5w, sbw):
    n, d = x2.shape
    wout = w.shape[1]
    kern = functools.partial(_in_proj_kernel, s5w=s5w, sbw=sbw, scale=SB_HEAD_DIM ** -0.5)
    return pl.pallas_call(
        kern,
        grid=(n // ROW_TILE,),
        in_specs=[pl.BlockSpec((ROW_TILE, d), lambda i: (i, 0)),
                  _const_spec((d, wout)), _const_spec((1, wout))],
        out_specs=[pl.BlockSpec((ROW_TILE, s5w), lambda i: (i, 0)),
                   pl.BlockSpec((ROW_TILE, 3 * sbw), lambda i: (i, 0))],
        out_shape=[jax.ShapeDtypeStruct((n, s5w), F32),
                   jax.ShapeDtypeStruct((n, 3 * sbw), BF16)],
        compiler_params=_params("parallel"),
        name="in_proj",
    )(x2, w, b)


def _attn_kernel(q_ref, k_ref, v_ref, tri_ref, o_ref):
    i = pl.program_id(2)
    t = ATT_TILE
    tri = tri_ref[...]
    q = q_ref[...]
    lane = lax.broadcasted_iota(jnp.int32, (1, LANES), 1)
    row = lax.broadcasted_iota(jnp.int32, (t, t), 0)
    col = lax.broadcasted_iota(jnp.int32, (t, t), 1)
    strict = col < row

    def block(j, carry, acc, qh, head_lanes, diagonal):
        start = pl.multiple_of(j * t, t)
        kj = k_ref[pl.ds(start, t), :]
        vj = jnp.where(head_lanes, v_ref[pl.ds(start, t), :], jnp.zeros((), BF16))
        z = lax.dot_general(qh, kj, (((1,), (1,)), ((), ())), preferred_element_type=F32)
        sp = jnp.maximum(z, 0.0) + jnp.log(1.0 + jnp.exp(-jnp.abs(z)))
        spm = jnp.where(strict, sp, 0.0) if diagonal else sp
        hi = spm.astype(BF16)
        lo = (spm - hi.astype(F32)).astype(BF16)
        cum = (jnp.dot(hi, tri, preferred_element_type=F32)
               + jnp.dot(lo, tri, preferred_element_type=F32))
        w = jnp.exp(z - sp - cum - carry)
        if diagonal:
            w = jnp.where(strict, w, 0.0)
        acc = acc + jnp.dot(w.astype(BF16), vj, preferred_element_type=F32)
        carry = carry + cum[:, 0:1] + spm[:, 0:1]
        return carry, acc

    acc = jnp.zeros((t, LANES), F32)
    for hh in range(LANES // SB_HEAD_DIM):
        head_lanes = (lane >= hh * SB_HEAD_DIM) & (lane < (hh + 1) * SB_HEAD_DIM)
        qh = jnp.where(head_lanes, q, jnp.zeros((), BF16))
        carry, acc = block(i, jnp.zeros((t, 1), F32), acc, qh, head_lanes, True)

        def body(jj, c, qh=qh, head_lanes=head_lanes):
            return block(i - 1 - jj, c[0], c[1], qh, head_lanes, False)

        carry, acc = lax.fori_loop(0, i, body, (carry, acc))
    o_ref[...] = acc.astype(BF16)


def _attention(qkv3, sbw):
    bsz, seq, _ = qkv3.shape
    ncol = sbw // LANES
    t = ATT_TILE
    tri = (lax.broadcasted_iota(jnp.int32, (t, t), 0)
           > lax.broadcasted_iota(jnp.int32, (t, t), 1)).astype(BF16)
    return pl.pallas_call(
        _attn_kernel,
        grid=(bsz, ncol, seq // t),
        in_specs=[pl.BlockSpec((None, t, LANES), lambda b, c, i: (b, i, c)),
                  pl.BlockSpec((None, seq, LANES), lambda b, c, i: (b, 0, ncol + c)),
                  pl.BlockSpec((None, seq, LANES), lambda b, c, i: (b, 0, 2 * ncol + c)),
                  _const_spec((t, t))],
        out_specs=pl.BlockSpec((None, t, LANES), lambda b, c, i: (b, i, c)),
        out_shape=jax.ShapeDtypeStruct((bsz, seq, sbw), BF16),
        compiler_params=_params("parallel", "parallel", "arbitrary"),
        name="stickbreak_attn",
    )(qkv3, qkv3, qkv3, tri)


def _s5_tables(lam_re, lam_im, log_dt, b_re, b_im, c_re, c_im):
    g, n = lam_re.shape
    p = b_re.shape[-1]
    gh = g // 2
    lr = jnp.minimum(lam_re.astype(F32), -1e-4)
    li = lam_im.astype(F32)
    dt = jnp.exp(log_dt.astype(F32))[:, None]
    mag = jnp.exp(lr * dt)
    lb_re = mag * jnp.cos(li * dt)
    lb_im = mag * jnp.sin(li * dt)
    den = lr * lr + li * li
    nr, ni = lb_re - 1.0, lb_im
    cf_re = (nr * lr + ni * li) / den
    cf_im = (ni * lr - nr * li) / den
    bb_re = cf_re[..., None] * b_re - cf_im[..., None] * b_im
    bb_im = cf_re[..., None] * b_im + cf_im[..., None] * b_re
    delta = jnp.eye(gh, dtype=F32).reshape(gh, gh // 2, 2)

    tb = jnp.stack([bb_re, bb_im], 0).transpose(1, 3, 0, 2)
    tb = tb.reshape(2, gh, p, 2, n)
    bm = (tb[:, :, :, None, :, None, :]
          * delta[None, :, None, :, None, :, None])
    bm = bm.reshape(2, gh * p, gh * 2 * n)

    tc = jnp.stack([c_re, -c_im], 0).astype(F32).transpose(1, 0, 3, 2)
    tc = tc.reshape(2, gh, 2, n, p).transpose(0, 2, 3, 1, 4)
    cm = (tc[:, None, :, None, :, :, :]
          * delta.transpose(1, 2, 0)[None, :, None, :, None, :, None])
    cm = cm.reshape(2, gh * 2 * n, gh * p)

    def cols(v):
        v = v.reshape(2, gh // 2, 1, 2, n)
        return jnp.broadcast_to(v, (2, gh // 2, 2, 2, n)).reshape(1, -1)

    coef = jnp.concatenate([jnp.broadcast_to(cols(lb_re), (SUBLANES, 2 * g * n)),
                            jnp.broadcast_to(cols(lb_im), (SUBLANES, 2 * g * n))], 0)
    return bm.astype(BF16), cm.astype(BF16), coef


def _gelu_tanh(x):
    return 0.5 * x * (1.0 + jnp.tanh(math.sqrt(2.0 / math.pi) * (x + 0.044715 * (x * x * x))))


def _s5_kernel(u_ref, bm_ref, cm_ref, coef_ref, d_ref, wg_ref, bg_ref, o_ref,
               utb_ref, st_ref, h_ref, otb_ref, *, nb, tt, width):
    rows = tt * nb
    nslab = width // LANES
    ncols = st_ref.shape[1]
    half = ncols // 2
    tiles = nb // SUBLANES

    @pl.when(pl.program_id(0) == 0)
    def _():
        h_ref[...] = jnp.zeros_like(h_ref)

    for b in range(nb):
        for s in range(nslab):
            utb_ref[s, pl.ds(b, tt, stride=nb), :] = u_ref[b, :, s * LANES:(s + 1) * LANES]
    u_tb = jnp.concatenate([utb_ref[s] for s in range(nslab)], axis=1)

    hw = width // 2
    for hf in range(2):
        st_ref[:, hf * half:(hf + 1) * half] = _bdot(u_tb[:, hf * hw:(hf + 1) * hw], bm_ref[hf])

    pairs = 4
    for cg in range(ncols // (2 * LANES * pairs)):
        base = cg * 2 * LANES * pairs
        re_c = [base + q * 2 * LANES for q in range(pairs)]
        im_c = [c + LANES for c in re_c]
        a_re = [coef_ref[0:SUBLANES, c:c + LANES] for c in re_c]
        a_im = [coef_ref[SUBLANES:2 * SUBLANES, c:c + LANES] for c in re_c]
        h0 = tuple(h_ref[s * SUBLANES:(s + 1) * SUBLANES, c:c + LANES]
                   for q in range(pairs) for c in (re_c[q], im_c[q]) for s in range(tiles))

        def step(t, h, re_c=re_c, im_c=im_c, a_re=a_re, a_im=a_im):
            out = []
            r0 = pl.multiple_of(t * nb, nb)
            for q in range(pairs):
                for s in range(tiles):
                    rs = pl.ds(r0 + s * SUBLANES, SUBLANES)
                    hr = h[(2 * q) * tiles + s]
                    hi = h[(2 * q + 1) * tiles + s]
                    nr = a_re[q] * hr - a_im[q] * hi + st_ref[rs, re_c[q]:re_c[q] + LANES]
                    ni = a_re[q] * hi + a_im[q] * hr + st_ref[rs, im_c[q]:im_c[q] + LANES]
                    st_ref[rs, re_c[q]:re_c[q] + LANES] = nr
                    st_ref[rs, im_c[q]:im_c[q] + LANES] = ni
                    out.append((q, 0, s, nr))
                    out.append((q, 1, s, ni))
            res = [None] * (2 * pairs * tiles)
            for q, part, s, v in out:
                res[(2 * q + part) * tiles + s] = v
            return tuple(res)

        hl = lax.fori_loop(0, tt, step, h0, unroll=4)
        k = 0
        for q in range(pairs):
            for c in (re_c[q], im_c[q]):
                for s in range(tiles):
                    h_ref[s * SUBLANES:(s + 1) * SUBLANES, c:c + LANES] = hl[k]
                    k += 1

    y = jnp.concatenate(
        [_bdot(st_ref[:, hf * half:(hf + 1) * half], cm_ref[hf]) for hf in range(2)], axis=1)
    y = _gelu_tanh(y + d_ref[...] * u_tb)
    hg = _bdot(y, wg_ref[...]) + bg_ref[...]
    out = hg[:, :width] * jax.nn.sigmoid(hg[:, width:])
    for s in range(nslab):
        otb_ref[s] = out[:, s * LANES:(s + 1) * LANES]
    for b in range(nb):
        o_ref[b] = jnp.concatenate(
            [otb_ref[s, pl.ds(b, tt, stride=nb), :] for s in range(nslab)], axis=1).astype(BF16)


def _s5(u3, bm, cm, coef, d, wg, bg):
    nb, seq, width = u3.shape
    tt = S5_TIME
    ncols = coef.shape[1]
    kern = functools.partial(_s5_kernel, nb=nb, tt=tt, width=width)
    return pl.pallas_call(
        kern,
        grid=(seq // tt,),
        in_specs=[pl.BlockSpec((nb, tt, width), lambda i: (0, i, 0)),
                  _const_spec(bm.shape), _const_spec(cm.shape), _const_spec(coef.shape),
                  _const_spec(d.shape), _const_spec(wg.shape), _const_spec(bg.shape)],
        out_specs=pl.BlockSpec((nb, tt, width), lambda i: (0, i, 0)),
        out_shape=jax.ShapeDtypeStruct((nb, seq, width), BF16),
        scratch_shapes=[pltpu.VMEM((width // LANES, tt * nb, LANES), F32),
                        pltpu.VMEM((tt * nb, ncols), F32),
                        pltpu.VMEM((nb, ncols), F32),
                        pltpu.VMEM((width // LANES, tt * nb, LANES), F32)],
        compiler_params=_params("arbitrary"),
        name="s5_mixer",
    )(u3, bm, cm, coef, d, wg, bg)


def _out_proj_kernel(x_ref, a_ref, s_ref, w_ref, b_ref, g_ref, be_ref, o_ref, *, s5w):
    mix = (jnp.dot(a_ref[...], w_ref[:s5w, :], preferred_element_type=F32)
           + jnp.dot(s_ref[...], w_ref[s5w:, :], preferred_element_type=F32) + b_ref[...])
    o_ref[...] = _layer_norm(ALPHA * x_ref[...] + mix, g_ref[...], be_ref[...])


def _out_proj(x2, s5o, sbo, w, b, g, be):
    n, d = x2.shape
    s5w, sbw = s5o.shape[1], sbo.shape[1]
    row = lambda width: pl.BlockSpec((ROW_TILE, width), lambda i: (i, 0))
    return pl.pallas_call(
        functools.partial(_out_proj_kernel, s5w=s5w),
        grid=(n // ROW_TILE,),
        in_specs=[row(d), row(s5w), row(sbw), _const_spec(w.shape),
                  _const_spec((1, d)), _const_spec((1, d)), _const_spec((1, d))],
        out_specs=row(d),
        out_shape=jax.ShapeDtypeStruct((n, d), F32),
        compiler_params=_params("parallel"),
        name="out_proj_ln",
    )(x2, s5o, sbo, w, b, g, be)


def _ffn_kernel(x_ref, w1_ref, b1_ref, w2_ref, b2_ref, g_ref, be_ref, o_ref, *, chunk):
    x = x_ref[...]
    xb = x.astype(BF16)
    acc = ALPHA * x + b2_ref[...]
    for c in range(w1_ref.shape[1] // chunk):
        cs = slice(c * chunk, (c + 1) * chunk)
        h = jnp.dot(xb, w1_ref[:, cs], preferred_element_type=F32) + b1_ref[:, cs]
        h = jnp.square(jnp.maximum(h, 0.0))
        acc = acc + jnp.dot(h.astype(BF16), w2_ref[cs, :], preferred_element_type=F32)
    o_ref[...] = _layer_norm(acc, g_ref[...], be_ref[...])


def _ffn(x2, w1, b1, w2, b2, g, be):
    n, d = x2.shape
    hid = w1.shape[1]
    row = pl.BlockSpec((ROW_TILE, d), lambda i: (i, 0))
    return pl.pallas_call(
        functools.partial(_ffn_kernel, chunk=1024),
        grid=(n // ROW_TILE,),
        in_specs=[row, _const_spec((d, hid)), _const_spec((1, hid)), _const_spec((hid, d)),
                  _const_spec((1, d)), _const_spec((1, d)), _const_spec((1, d))],
        out_specs=row,
        out_shape=jax.ShapeDtypeStruct((n, d), F32),
        compiler_params=_params("parallel"),
        name="ffn_ln",
    )(x2, w1, b1, w2, b2, g, be)


def _conv_kernel(x_ref, w1_ref, b1_ref, wd_ref, bd_ref, cg_ref, cb_ref, w2_ref, b2_ref,
                 g_ref, be_ref, o_ref, buf_ref, cv_ref, *, width):
    t = CONV_TILE

    @pl.when(pl.program_id(1) == 0)
    def _():
        buf_ref[0:CONV_HIST, :] = jnp.zeros((CONV_HIST, width), F32)

    x = x_ref[...]
    h = _bdot(x, w1_ref[...]) + b1_ref[...]
    buf_ref[CONV_HIST:CONV_HIST + t, :] = h[:, :width] * jax.nn.sigmoid(h[:, width:])

    first = CONV_HIST - (CONV_SIZE - 1)
    rchunk = 64
    for cb in range(width // LANES):
        cs = slice(cb * LANES, (cb + 1) * LANES)
        for r in range(t // rchunk):
            acc = jnp.broadcast_to(bd_ref[:, cs], (rchunk, LANES))
            for k in range(CONV_SIZE):
                acc = acc + wd_ref[k:k + 1, cs] * buf_ref[pl.ds(first + k + r * rchunk, rchunk), cs]
            cv_ref[r * rchunk:(r + 1) * rchunk, cs] = acc
    buf_ref[0:CONV_HIST, :] = buf_ref[t:t + CONV_HIST, :]

    c = _layer_norm(cv_ref[...], cg_ref[...], cb_ref[...])
    c = c * jax.nn.sigmoid(c)
    mix = _bdot(c, w2_ref[...]) + b2_ref[...]
    o_ref[...] = _layer_norm(ALPHA * x + mix, g_ref[...], be_ref[...])


def _conv_layer(x3, w1, b1, wd, bd, cg, cb, w2, b2, g, be):
    bsz, seq, d = x3.shape
    width = w2.shape[0]
    t = CONV_TILE
    blk = pl.BlockSpec((None, t, d), lambda b, i: (b, i, 0))
    return pl.pallas_call(
        functools.partial(_conv_kernel, width=width),
        grid=(bsz, seq // t),
        in_specs=[blk, _const_spec(w1.shape), _const_spec(b1.shape), _const_spec(wd.shape),
                  _const_spec(bd.shape), _const_spec(cg.shape), _const_spec(cb.shape),
                  _const_spec(w2.shape), _const_spec(b2.shape),
                  _const_spec(g.shape), _const_spec(be.shape)],
        out_specs=blk,
        out_shape=jax.ShapeDtypeStruct((bsz, seq, d), F32),
        scratch_shapes=[pltpu.VMEM((CONV_HIST + t, width), F32),
                        pltpu.VMEM((t, width), F32)],
        compiler_params=_params("parallel", "arbitrary"),
        name="conformer_conv_ln",
    )(x3, w1, b1, wd, bd, cg, cb, w2, b2, g, be)


def _row(v):
    return v.astype(F32).reshape(1, -1)


def kernel(x, ln1_g, ln1_b, ln2_g, ln2_b, ffn_w1, ffn_b1, ffn_w2, ffn_b2, mix_w_in, mix_b_in, s5_lambda_re, s5_lambda_im, s5_log_dt, s5_b_re, s5_b_im, s5_c_re, s5_c_im, s5_d, s5_w_glu, s5_b_glu, mix_w_out, mix_b_out, conv_w_pw1, conv_b_pw1, conv_w_dw, conv_b_dw, conv_ln_g, conv_ln_b, conv_w_pw2, conv_b_pw2):
    bsz, seq, d = x.shape
    depth = ln1_g.shape[0]
    s5w = s5_d.shape[1]
    sbw = (mix_w_in.shape[2] - s5w) // 3
    n = bsz * seq
    x2 = x.reshape(n, d)
    for layer in range(depth):
        i = layer // 2
        if layer % 2 == 0:
            u, qkv = _in_proj(x2, mix_w_in[i].astype(BF16), _row(mix_b_in[i]), s5w, sbw)
            sbo = _attention(qkv.reshape(bsz, seq, 3 * sbw), sbw)
            bm, cm, coef = _s5_tables(s5_lambda_re[i], s5_lambda_im[i], s5_log_dt[i],
                                      s5_b_re[i], s5_b_im[i], s5_c_re[i], s5_c_im[i])
            s5o = _s5(u.reshape(bsz, seq, s5w), bm, cm, coef, _row(s5_d[i]),
                      s5_w_glu[i].astype(BF16), _row(s5_b_glu[i]))
            x2 = _out_proj(x2, s5o.reshape(n, s5w), sbo.reshape(n, sbw),
                           mix_w_out[i].astype(BF16), _row(mix_b_out[i]),
                           _row(ln1_g[layer]), _row(ln1_b[layer]))
        else:
            x2 = _conv_layer(x2.reshape(bsz, seq, d), conv_w_pw1[i].astype(BF16),
                             _row(conv_b_pw1[i]), conv_w_dw[i].astype(F32), _row(conv_b_dw[i]),
                             _row(conv_ln_g[i]), _row(conv_ln_b[i]),
                             conv_w_pw2[i].astype(BF16), _row(conv_b_pw2[i]),
                             _row(ln1_g[layer]), _row(ln1_b[layer])).reshape(n, d)
        x2 = _ffn(x2, ffn_w1[layer].astype(BF16), _row(ffn_b1[layer]),
                  ffn_w2[layer].astype(BF16), _row(ffn_b2[layer]),
                  _row(ln2_g[layer]), _row(ln2_b[layer]))
    return x2.reshape(bsz, seq, d)
```

```python
import functools
import math

import jax
import jax.numpy as jnp
from jax import lax
from jax.experimental import pallas as pl
from jax.experimental.pallas import tpu as pltpu

F32 = jnp.float32
BF16 = jnp.bfloat16

LANES = 128
SUBLANES = 8
VMEM_LIMIT = 56 * 1024 * 1024

S5_GROUP = 16
S5_STATE = 64
SB_HEAD_DIM = 64
CONV_SIZE = 31
LN_EPS = 1e-5
DEPTH = 2
ALPHA = (2 * DEPTH) ** 0.25

ROW_TILE = 512
ATT_TILE = 256
ATT_HEADS = 4
MASK_BIAS = -1e30
S5_TIME = 64
CONV_TILE = 256
CONV_HIST = 32


def _params(*sem):
    return pltpu.CompilerParams(dimension_semantics=sem, vmem_limit_bytes=VMEM_LIMIT)


def _const_spec(shape):
    nd = len(shape)
    return pl.BlockSpec(shape, lambda *_: (0,) * nd, pipeline_mode=pl.Buffered(1))


def _layer_norm(r, g, b):
    mu = jnp.mean(r, axis=-1, keepdims=True)
    c = r - mu
    var = jnp.mean(c * c, axis=-1, keepdims=True)
    return c * lax.rsqrt(var + LN_EPS) * g + b


def _bdot(a, w):
    return jnp.dot(a.astype(BF16), w, preferred_element_type=F32)


def _in_proj_kernel(x_ref, w_ref, b_ref, u_ref, qkv_ref, *, s5w, sbw, scale):
    h = _bdot(x_ref[...], w_ref[...]) + b_ref[...]
    u_ref[...] = h[:, :s5w]
    qkv_ref[:, :sbw] = (h[:, s5w:s5w + sbw] * scale).astype(BF16)
    qkv_ref[:, sbw:] = h[:, s5w + sbw:].astype(BF16)


def _in_proj(x2, w, b, s5w, sbw):
    n, d = x2.shape
    wout = w.shape[1]
    kern = functools.partial(_in_proj_kernel, s5w=s5w, sbw=sbw,
                             scale=SB_HEAD_DIM ** -0.5 * math.log2(math.e))
    return pl.pallas_call(
        kern,
        grid=(n // ROW_TILE,),
        in_specs=[pl.BlockSpec((ROW_TILE, d), lambda i: (i, 0)),
                  _const_spec((d, wout)), _const_spec((1, wout))],
        out_specs=[pl.BlockSpec((ROW_TILE, s5w), lambda i: (i, 0)),
                   pl.BlockSpec((ROW_TILE, 3 * sbw), lambda i: (i, 0))],
        out_shape=[jax.ShapeDtypeStruct((n, s5w), F32),
                   jax.ShapeDtypeStruct((n, 3 * sbw), BF16)],
        compiler_params=_params("parallel"),
        name="in_proj",
    )(x2, w, b)


def _attn_kernel(q_ref, k_ref, v_ref, tri_ref, o_ref, z_ref, s_ref, *, heads):
    i = pl.program_id(2)
    t = ATT_TILE
    width = q_ref.shape[1]
    tri2 = tri_ref[...]
    q = q_ref[...]
    lane = lax.broadcasted_iota(jnp.int32, (1, width), 1)
    masks = [(lane >= h * SB_HEAD_DIM) & (lane < (h + 1) * SB_HEAD_DIM) for h in range(heads)]
    zero = jnp.zeros((), BF16)
    qs = [jnp.where(m, q, zero) for m in masks]
    sign = jnp.uint32(0x80000000)

    def scores(j, diagonal):
        kj = k_ref[pl.ds(pl.multiple_of(j * t, t), t), :]
        for h in range(heads):
            z = lax.dot_general(qs[h], kj, (((1,), (1,)), ((), ())), preferred_element_type=F32)
            if diagonal:
                row = lax.broadcasted_iota(jnp.int32, (t, t), 0)
                col = lax.broadcasted_iota(jnp.int32, (t, t), 1)
                z = z + jnp.where(col < row, 0.0, MASK_BIAS)
            neg_abs = lax.bitcast_convert_type(lax.bitcast_convert_type(z, jnp.uint32) | sign, F32)
            sp = jnp.maximum(z, 0.0) + jnp.log2(1.0 + jnp.exp2(neg_abs))
            hi = sp.astype(BF16)
            z_ref[h] = z
            s_ref[h, :, 0:t] = hi
            s_ref[h, :, t:2 * t] = (sp - hi.astype(F32)).astype(BF16)

    def accumulate(j, carries, acc):
        vj = v_ref[pl.ds(pl.multiple_of(j * t, t), t), :]
        ws, new_carries = [], []
        for h in range(heads):
            incl = jnp.dot(s_ref[h], tri2, preferred_element_type=F32)
            ws.append(jnp.exp2((z_ref[h] - carries[h]) - incl).astype(BF16))
            new_carries.append(carries[h] + incl[:, 0:1])
        vcat = jnp.concatenate([jnp.where(m, vj, zero) for m in masks], axis=0)
        acc = acc + jnp.dot(jnp.concatenate(ws, axis=1), vcat, preferred_element_type=F32)
        return tuple(new_carries), acc

    scores(i, True)

    def body(jj, c):
        j = i - 1 - jj
        c = accumulate(j + 1, *c)
        scores(j, False)
        return c

    init = (tuple(jnp.zeros((t, 1), F32) for _ in range(heads)), jnp.zeros((t, width), F32))
    carries, acc = lax.fori_loop(0, i, body, init)
    o_ref[...] = accumulate(0, carries, acc)[1].astype(BF16)


def _attention(qkv3, sbw):
    bsz, seq, _ = qkv3.shape
    width = ATT_HEADS * SB_HEAD_DIM
    ncol = sbw // width
    t = ATT_TILE
    tri = (lax.broadcasted_iota(jnp.int32, (2 * t, t), 0) % t
           >= lax.broadcasted_iota(jnp.int32, (2 * t, t), 1)).astype(BF16)
    return pl.pallas_call(
        functools.partial(_attn_kernel, heads=ATT_HEADS),
        grid=(bsz, ncol, seq // t),
        in_specs=[pl.BlockSpec((None, t, width), lambda b, c, i: (b, i, c)),
                  pl.BlockSpec((None, seq, width), lambda b, c, i: (b, 0, ncol + c)),
                  pl.BlockSpec((None, seq, width), lambda b, c, i: (b, 0, 2 * ncol + c)),
                  _const_spec((2 * t, t))],
        out_specs=pl.BlockSpec((None, t, width), lambda b, c, i: (b, i, c)),
        out_shape=jax.ShapeDtypeStruct((bsz, seq, sbw), BF16),
        scratch_shapes=[pltpu.VMEM((ATT_HEADS, t, t), F32),
                        pltpu.VMEM((ATT_HEADS, t, 2 * t), BF16)],
        compiler_params=_params("parallel", "parallel", "arbitrary"),
        name="stickbreak_attn",
    )(qkv3, qkv3, qkv3, tri)


def _s5_tables(lam_re, lam_im, log_dt, b_re, b_im, c_re, c_im):
    g, n = lam_re.shape
    p = b_re.shape[-1]
    gh = g // 2
    lr = jnp.minimum(lam_re.astype(F32), -1e-4)
    li = lam_im.astype(F32)
    dt = jnp.exp(log_dt.astype(F32))[:, None]
    mag = jnp.exp(lr * dt)
    lb_re = mag * jnp.cos(li * dt)
    lb_im = mag * jnp.sin(li * dt)
    den = lr * lr + li * li
    nr, ni = lb_re - 1.0, lb_im
    cf_re = (nr * lr + ni * li) / den
    cf_im = (ni * lr - nr * li) / den
    bb_re = cf_re[..., None] * b_re - cf_im[..., None] * b_im
    bb_im = cf_re[..., None] * b_im + cf_im[..., None] * b_re
    delta = jnp.eye(gh, dtype=F32).reshape(gh, gh // 2, 2)

    tb = jnp.stack([bb_re, bb_im], 0).transpose(1, 3, 0, 2)
    tb = tb.reshape(2, gh, p, 2, n)
    bm = (tb[:, :, :, None, :, None, :]
          * delta[None, :, None, :, None, :, None])
    bm = bm.reshape(2, gh * p, gh * 2 * n)

    tc = jnp.stack([c_re, -c_im], 0).astype(F32).transpose(1, 0, 3, 2)
    tc = tc.reshape(2, gh, 2, n, p).transpose(0, 2, 3, 1, 4)
    cm = (tc[:, None, :, None, :, :, :]
          * delta.transpose(1, 2, 0)[None, :, None, :, None, :, None])
    cm = cm.reshape(2, gh * 2 * n, gh * p)

    def cols(v):
        v = v.reshape(2, gh // 2, 1, 2, n)
        return jnp.broadcast_to(v, (2, gh // 2, 2, 2, n)).reshape(1, -1)

    coef = jnp.concatenate([jnp.broadcast_to(cols(lb_re), (SUBLANES, 2 * g * n)),
                            jnp.broadcast_to(cols(lb_im), (SUBLANES, 2 * g * n))], 0)
    return bm.astype(BF16), cm.astype(BF16), coef


def _gelu_tanh(x):
    return 0.5 * x * (1.0 + jnp.tanh(math.sqrt(2.0 / math.pi) * (x + 0.044715 * (x * x * x))))


def _s5_kernel(u_ref, bm_ref, cm_ref, coef_ref, d_ref, wg_ref, bg_ref, o_ref,
               utb_ref, bu_ref, st_ref, h_ref, otb_ref, *, nb, tt, width):
    rows = tt * nb
    nslab = width // LANES
    ncols = st_ref.shape[1]
    half = ncols // 2
    tiles = nb // SUBLANES

    @pl.when(pl.program_id(0) == 0)
    def _():
        h_ref[...] = jnp.zeros_like(h_ref)

    for b in range(nb):
        for s in range(nslab):
            utb_ref[s, pl.ds(b, tt, stride=nb), :] = u_ref[b, :, s * LANES:(s + 1) * LANES]
    u_tb = jnp.concatenate([utb_ref[s] for s in range(nslab)], axis=1)

    hw = width // 2
    for hf in range(2):
        bu_ref[:, hf * half:(hf + 1) * half] = _bdot(u_tb[:, hf * hw:(hf + 1) * hw], bm_ref[hf])

    pairs = 4
    for cg in range(ncols // (2 * LANES * pairs)):
        base = cg * 2 * LANES * pairs
        re_c = [base + q * 2 * LANES for q in range(pairs)]
        im_c = [c + LANES for c in re_c]
        a_re = [coef_ref[0:SUBLANES, c:c + LANES] for c in re_c]
        a_im = [coef_ref[SUBLANES:2 * SUBLANES, c:c + LANES] for c in re_c]
        h0 = tuple(h_ref[s * SUBLANES:(s + 1) * SUBLANES, c:c + LANES]
                   for q in range(pairs) for c in (re_c[q], im_c[q]) for s in range(tiles))

        def step(t, h, re_c=re_c, im_c=im_c, a_re=a_re, a_im=a_im):
            out = []
            r0 = pl.multiple_of(t * nb, nb)
            for q in range(pairs):
                for s in range(tiles):
                    rs = pl.ds(r0 + s * SUBLANES, SUBLANES)
                    hr = h[(2 * q) * tiles + s]
                    hi = h[(2 * q + 1) * tiles + s]
                    nr = a_re[q] * hr - a_im[q] * hi + bu_ref[rs, re_c[q]:re_c[q] + LANES]
                    ni = a_re[q] * hi + a_im[q] * hr + bu_ref[rs, im_c[q]:im_c[q] + LANES]
                    st_ref[rs, re_c[q]:re_c[q] + LANES] = nr
                    st_ref[rs, im_c[q]:im_c[q] + LANES] = ni
                    out.append((q, 0, s, nr))
                    out.append((q, 1, s, ni))
            res = [None] * (2 * pairs * tiles)
            for q, part, s, v in out:
                res[(2 * q + part) * tiles + s] = v
            return tuple(res)

        hl = lax.fori_loop(0, tt, step, h0, unroll=4)
        k = 0
        for q in range(pairs):
            for c in (re_c[q], im_c[q]):
                for s in range(tiles):
                    h_ref[s * SUBLANES:(s + 1) * SUBLANES, c:c + LANES] = hl[k]
                    k += 1

    y = jnp.concatenate(
        [_bdot(st_ref[:, hf * half:(hf + 1) * half], cm_ref[hf]) for hf in range(2)], axis=1)
    y = _gelu_tanh(y + d_ref[...] * u_tb)
    hg = _bdot(y, wg_ref[...]) + bg_ref[...]
    out = hg[:, :width] * jax.nn.sigmoid(hg[:, width:])
    for s in range(nslab):
        otb_ref[s] = out[:, s * LANES:(s + 1) * LANES]
    for b in range(nb):
        o_ref[b] = jnp.concatenate(
            [otb_ref[s, pl.ds(b, tt, stride=nb), :] for s in range(nslab)], axis=1).astype(BF16)


def _s5(u3, bm, cm, coef, d, wg, bg):
    nb, seq, width = u3.shape
    tt = S5_TIME
    ncols = coef.shape[1]
    kern = functools.partial(_s5_kernel, nb=nb, tt=tt, width=width)
    return pl.pallas_call(
        kern,
        grid=(seq // tt,),
        in_specs=[pl.BlockSpec((nb, tt, width), lambda i: (0, i, 0)),
                  _const_spec(bm.shape), _const_spec(cm.shape), _const_spec(coef.shape),
                  _const_spec(d.shape), _const_spec(wg.shape), _const_spec(bg.shape)],
        out_specs=pl.BlockSpec((nb, tt, width), lambda i: (0, i, 0)),
        out_shape=jax.ShapeDtypeStruct((nb, seq, width), BF16),
        scratch_shapes=[pltpu.VMEM((width // LANES, tt * nb, LANES), F32),
                        pltpu.VMEM((tt * nb, ncols), F32),
                        pltpu.VMEM((tt * nb, ncols), F32),
                        pltpu.VMEM((nb, ncols), F32),
                        pltpu.VMEM((width // LANES, tt * nb, LANES), F32)],
        compiler_params=_params("arbitrary"),
        name="s5_mixer",
    )(u3, bm, cm, coef, d, wg, bg)


def _out_proj_kernel(x_ref, a_ref, s_ref, w_ref, b_ref, g_ref, be_ref, o_ref, *, s5w):
    mix = (jnp.dot(a_ref[...], w_ref[:s5w, :], preferred_element_type=F32)
           + jnp.dot(s_ref[...], w_ref[s5w:, :], preferred_element_type=F32) + b_ref[...])
    o_ref[...] = _layer_norm(ALPHA * x_ref[...] + mix, g_ref[...], be_ref[...])


def _out_proj(x2, s5o, sbo, w, b, g, be):
    n, d = x2.shape
    s5w, sbw = s5o.shape[1], sbo.shape[1]
    row = lambda width: pl.BlockSpec((ROW_TILE, width), lambda i: (i, 0))
    return pl.pallas_call(
        functools.partial(_out_proj_kernel, s5w=s5w),
        grid=(n // ROW_TILE,),
        in_specs=[row(d), row(s5w), row(sbw), _const_spec(w.shape),
                  _const_spec((1, d)), _const_spec((1, d)), _const_spec((1, d))],
        out_specs=row(d),
        out_shape=jax.ShapeDtypeStruct((n, d), F32),
        compiler_params=_params("parallel"),
        name="out_proj_ln",
    )(x2, s5o, sbo, w, b, g, be)


def _ffn_kernel(x_ref, w1_ref, b1_ref, w2_ref, b2_ref, g_ref, be_ref, o_ref, *, chunk):
    x = x_ref[...]
    xb = x.astype(BF16)
    acc = ALPHA * x + b2_ref[...]
    for c in range(w1_ref.shape[1] // chunk):
        cs = slice(c * chunk, (c + 1) * chunk)
        h = jnp.dot(xb, w1_ref[:, cs], preferred_element_type=F32) + b1_ref[:, cs]
        h = jnp.square(jnp.maximum(h, 0.0))
        acc = acc + jnp.dot(h.astype(BF16), w2_ref[cs, :], preferred_element_type=F32)
    o_ref[...] = _layer_norm(acc, g_ref[...], be_ref[...])


def _ffn(x2, w1, b1, w2, b2, g, be):
    n, d = x2.shape
    hid = w1.shape[1]
    row = pl.BlockSpec((ROW_TILE, d), lambda i: (i, 0))
    return pl.pallas_call(
        functools.partial(_ffn_kernel, chunk=1024),
        grid=(n // ROW_TILE,),
        in_specs=[row, _const_spec((d, hid)), _const_spec((1, hid)), _const_spec((hid, d)),
                  _const_spec((1, d)), _const_spec((1, d)), _const_spec((1, d))],
        out_specs=row,
        out_shape=jax.ShapeDtypeStruct((n, d), F32),
        compiler_params=_params("parallel"),
        name="ffn_ln",
    )(x2, w1, b1, w2, b2, g, be)


def _conv_kernel(x_ref, w1_ref, b1_ref, wd_ref, bd_ref, cg_ref, cb_ref, w2_ref, b2_ref,
                 g_ref, be_ref, o_ref, buf_ref, cv_ref, *, width):
    t = CONV_TILE

    nslab = width // LANES

    @pl.when(pl.program_id(1) == 0)
    def _():
        buf_ref[:, 0:CONV_HIST, :] = jnp.zeros((nslab, CONV_HIST, LANES), F32)

    x = x_ref[...]
    h = _bdot(x, w1_ref[...]) + b1_ref[...]
    glu = h[:, :width] * jax.nn.sigmoid(h[:, width:])
    for cb in range(nslab):
        buf_ref[cb, CONV_HIST:CONV_HIST + t, :] = glu[:, cb * LANES:(cb + 1) * LANES]

    first = CONV_HIST - (CONV_SIZE - 1)
    for cb in range(nslab):
        cs = slice(cb * LANES, (cb + 1) * LANES)
        for par in range(2):
            acc = jnp.broadcast_to(bd_ref[:, cs], (t // 2, LANES))
            for k in range(CONV_SIZE):
                acc = acc + wd_ref[k:k + 1, cs] * buf_ref[cb, pl.ds(first + k + par, t // 2, stride=2), :]
            cv_ref[cb, pl.ds(par, t // 2, stride=2), :] = acc
        buf_ref[cb, 0:CONV_HIST, :] = buf_ref[cb, t:t + CONV_HIST, :]

    conv = jnp.concatenate([cv_ref[cb] for cb in range(nslab)], axis=1)
    c = _layer_norm(conv, cg_ref[...], cb_ref[...])
    c = c * jax.nn.sigmoid(c)
    mix = _bdot(c, w2_ref[...]) + b2_ref[...]
    o_ref[...] = _layer_norm(ALPHA * x + mix, g_ref[...], be_ref[...])


def _conv_layer(x3, w1, b1, wd, bd, cg, cb, w2, b2, g, be):
    bsz, seq, d = x3.shape
    width = w2.shape[0]
    t = CONV_TILE
    blk = pl.BlockSpec((None, t, d), lambda b, i: (b, i, 0))
    return pl.pallas_call(
        functools.partial(_conv_kernel, width=width),
        grid=(bsz, seq // t),
        in_specs=[blk, _const_spec(w1.shape), _const_spec(b1.shape), _const_spec(wd.shape),
                  _const_spec(bd.shape), _const_spec(cg.shape), _const_spec(cb.shape),
                  _const_spec(w2.shape), _const_spec(b2.shape),
                  _const_spec(g.shape), _const_spec(be.shape)],
        out_specs=blk,
        out_shape=jax.ShapeDtypeStruct((bsz, seq, d), F32),
        scratch_shapes=[pltpu.VMEM((width // LANES, CONV_HIST + t, LANES), F32),
                        pltpu.VMEM((width // LANES, t, LANES), F32)],
        compiler_params=_params("parallel", "arbitrary"),
        name="conformer_conv_ln",
    )(x3, w1, b1, wd, bd, cg, cb, w2, b2, g, be)


def _row(v):
    return v.astype(F32).reshape(1, -1)


def kernel(x, ln1_g, ln1_b, ln2_g, ln2_b, ffn_w1, ffn_b1, ffn_w2, ffn_b2, mix_w_in, mix_b_in, s5_lambda_re, s5_lambda_im, s5_log_dt, s5_b_re, s5_b_im, s5_c_re, s5_c_im, s5_d, s5_w_glu, s5_b_glu, mix_w_out, mix_b_out, conv_w_pw1, conv_b_pw1, conv_w_dw, conv_b_dw, conv_ln_g, conv_ln_b, conv_w_pw2, conv_b_pw2):
    bsz, seq, d = x.shape
    depth = ln1_g.shape[0]
    s5w = s5_d.shape[1]
    sbw = (mix_w_in.shape[2] - s5w) // 3
    n = bsz * seq
    x2 = x.reshape(n, d)
    for layer in range(depth):
        i = layer // 2
        if layer % 2 == 0:
            u, qkv = _in_proj(x2, mix_w_in[i].astype(BF16), _row(mix_b_in[i]), s5w, sbw)
            sbo = _attention(qkv.reshape(bsz, seq, 3 * sbw), sbw)
            bm, cm, coef = _s5_tables(s5_lambda_re[i], s5_lambda_im[i], s5_log_dt[i],
                                      s5_b_re[i], s5_b_im[i], s5_c_re[i], s5_c_im[i])
            s5o = _s5(u.reshape(bsz, seq, s5w), bm, cm, coef, _row(s5_d[i]),
                      s5_w_glu[i].astype(BF16), _row(s5_b_glu[i]))
            x2 = _out_proj(x2, s5o.reshape(n, s5w), sbo.reshape(n, sbw),
                           mix_w_out[i].astype(BF16), _row(mix_b_out[i]),
                           _row(ln1_g[layer]), _row(ln1_b[layer]))
        else:
            x2 = _conv_layer(x2.reshape(bsz, seq, d), conv_w_pw1[i].astype(BF16),
                             _row(conv_b_pw1[i]), conv_w_dw[i].astype(F32), _row(conv_b_dw[i]),
                             _row(conv_ln_g[i]), _row(conv_ln_b[i]),
                             conv_w_pw2[i].astype(BF16), _row(conv_b_pw2[i]),
                             _row(ln1_g[layer]), _row(ln1_b[layer])).reshape(n, d)
        x2 = _ffn(x2, ffn_w1[layer].astype(BF16), _row(ffn_b1[layer]),
                  ffn_w2[layer].astype(BF16), _row(ffn_b2[layer]),
                  _row(ln2_g[layer]), _row(ln2_b[layer]))
    return x2.reshape(bsz, seq, d)
```

```python
import functools
import math

import numpy as np
import jax
import jax.numpy as jnp
from jax import lax
from jax.experimental import pallas as pl
from jax.experimental.pallas import tpu as pltpu

F32 = jnp.float32
BF16 = jnp.bfloat16

LANES = 128
SUBLANES = 8
VMEM_LIMIT = 56 * 1024 * 1024

S5_GROUP = 16
S5_STATE = 64
SB_HEAD_DIM = 64
CONV_SIZE = 31
LN_EPS = 1e-5
DEPTH = 2
ALPHA = (2 * DEPTH) ** 0.25

ROW_TILE = 512
FFN_CHUNK = 1024
ATT_TILE = 256
ATT_HEADS = 4
ATT_STAGES = 4
MASK_BIAS = -1e30
S5_TIME = 64
CONV_TILE = 256
CONV_HIST = 32


def _params(*sem):
    return pltpu.CompilerParams(dimension_semantics=sem, vmem_limit_bytes=VMEM_LIMIT)


def _const_spec(shape):
    nd = len(shape)
    return pl.BlockSpec(shape, lambda *_: (0,) * nd, pipeline_mode=pl.Buffered(1))


def _layer_norm(r, g, b):
    mu = jnp.mean(r, axis=-1, keepdims=True)
    c = r - mu
    var = jnp.mean(c * c, axis=-1, keepdims=True)
    return c * lax.rsqrt(var + LN_EPS) * g + b


def _bdot(a, w):
    return jnp.dot(a.astype(BF16), w, preferred_element_type=F32)


def _in_proj_kernel(x_ref, w_ref, b_ref, u_ref, qkv_ref, *, s5w, sbw, scale):
    h = _bdot(x_ref[...], w_ref[...]) + b_ref[...]
    u_ref[...] = h[:, :s5w]
    qkv_ref[:, :sbw] = (h[:, s5w:s5w + sbw] * scale).astype(BF16)
    qkv_ref[:, sbw:] = h[:, s5w + sbw:].astype(BF16)


def _in_proj(x2, w, b, s5w, sbw):
    n, d = x2.shape
    wout = w.shape[1]
    kern = functools.partial(_in_proj_kernel, s5w=s5w, sbw=sbw,
                             scale=SB_HEAD_DIM ** -0.5 * math.log2(math.e))
    return pl.pallas_call(
        kern,
        grid=(n // ROW_TILE,),
        in_specs=[pl.BlockSpec((ROW_TILE, d), lambda i: (i, 0)),
                  _const_spec((d, wout)), _const_spec((1, wout))],
        out_specs=[pl.BlockSpec((ROW_TILE, s5w), lambda i: (i, 0)),
                   pl.BlockSpec((ROW_TILE, 3 * sbw), lambda i: (i, 0))],
        out_shape=[jax.ShapeDtypeStruct((n, s5w), F32),
                   jax.ShapeDtypeStruct((n, 3 * sbw), BF16)],
        compiler_params=_params("parallel"),
        name="in_proj",
    )(x2, w, b)


def _attn_tables(nblk):
    lead = ATT_STAGES - 1
    tiles = [(0, 0, 2)] * lead
    for r in range(nblk):
        tiles += [(r, c, 1 if c == r else 0) for c in range(r, -1, -1)]
    steps = -(-len(tiles) // ATT_STAGES) * ATT_STAGES
    tiles += [(0, 0, 2)] * (steps + lead - len(tiles))
    return np.asarray(tiles, np.int32).T.copy(), steps


def _attn_kernel(tab_ref, q_ref, k_ref, v_ref, tri_ref, o_ref,
                 qm_ref, vm_ref, z_ref, s_ref, w_ref, bias_ref, acc_ref, *, heads, steps):
    t = ATT_TILE
    seq, width = q_ref.shape
    nblk = seq // t
    nt = (((1,), (1,)), ((), ()))
    lane = lax.broadcasted_iota(jnp.int32, (1, width), 1)
    masks = [(lane >= h * SB_HEAD_DIM) & (lane < (h + 1) * SB_HEAD_DIM) for h in range(heads)]
    zero = jnp.zeros((), BF16)
    for r in range(nblk):
        rows = slice(r * t, (r + 1) * t)
        qr = q_ref[rows, :]
        vr = v_ref[rows, :]
        for h in range(heads):
            qm_ref[h, rows, :] = jnp.where(masks[h], qr, zero)
            vm_ref[r, h * t:(h + 1) * t, :] = jnp.where(masks[h], vr, zero)
    row = lax.broadcasted_iota(jnp.int32, (t, t), 0)
    col = lax.broadcasted_iota(jnp.int32, (t, t), 1)
    bias_ref[0] = jnp.zeros((t, t), F32)
    bias_ref[1] = jnp.where(col < row, 0.0, MASK_BIAS)
    bias_ref[2] = jnp.full((t, t), MASK_BIAS, F32)
    z_ref[...] = jnp.full(z_ref.shape, MASK_BIAS, F32)
    s_ref[...] = jnp.zeros(s_ref.shape, BF16)
    w_ref[...] = jnp.zeros(w_ref.shape, BF16)
    acc_ref[...] = jnp.zeros(acc_ref.shape, F32)
    tri2 = tri_ref[...]

    def substep(n, r, carries):
        slot = r % ATT_STAGES
        rows = pl.ds(pl.multiple_of(tab_ref[0, n] * t, t), t)
        acc_ref[rows, :] += jnp.dot(w_ref[slot], vm_ref[tab_ref[1, n]],
                                    preferred_element_type=F32)
        slot = (r + 1) % ATT_STAGES
        row_start = tab_ref[2, n + 1] == 1
        new_carries = []
        for h in range(heads):
            c = jnp.where(row_start, 0.0, carries[h])
            incl = jnp.dot(s_ref[slot, h], tri2, preferred_element_type=F32)
            w_ref[slot, :, h * t:(h + 1) * t] = jnp.exp2((z_ref[slot, h] - c) - incl).astype(BF16)
            new_carries.append(c + incl[:, 0:1])
        slot = (r + 2) % ATT_STAGES
        for h in range(heads):
            z = z_ref[slot, h]
            sp = jnp.maximum(z, 0.0) + jnp.log2(1.0 + jnp.exp2(-jnp.abs(z)))
            hi = sp.astype(BF16)
            s_ref[slot, h, :, 0:t] = hi
            s_ref[slot, h, :, t:2 * t] = (sp - hi.astype(F32)).astype(BF16)
        slot = (r + 3) % ATT_STAGES
        kj = k_ref[pl.ds(pl.multiple_of(tab_ref[1, n + 3] * t, t), t), :]
        bias = bias_ref[tab_ref[2, n + 3]]
        qrows = pl.ds(pl.multiple_of(tab_ref[0, n + 3] * t, t), t)
        for h in range(heads):
            z_ref[slot, h] = lax.dot_general(qm_ref[h, qrows, :], kj, nt,
                                             preferred_element_type=F32) + bias
        return tuple(new_carries)

    def body(it, carries):
        for r in range(ATT_STAGES):
            carries = substep(ATT_STAGES * it + r, r, carries)
        return carries

    lax.fori_loop(0, steps // ATT_STAGES, body,
                  tuple(jnp.zeros((t, 1), F32) for _ in range(heads)))
    o_ref[...] = acc_ref[...].astype(BF16)


def _attention(qkv3, sbw):
    bsz, seq, _ = qkv3.shape
    width = ATT_HEADS * SB_HEAD_DIM
    ncol = sbw // width
    t = ATT_TILE
    nblk = seq // t
    table, steps = _attn_tables(nblk)
    tri = (lax.broadcasted_iota(jnp.int32, (2 * t, t), 0) % t
           >= lax.broadcasted_iota(jnp.int32, (2 * t, t), 1)).astype(BF16)
    grid_spec = pltpu.PrefetchScalarGridSpec(
        num_scalar_prefetch=1,
        grid=(bsz, ncol),
        in_specs=[pl.BlockSpec((None, seq, width), lambda b, c, tab: (b, 0, c)),
                  pl.BlockSpec((None, seq, width), lambda b, c, tab: (b, 0, ncol + c)),
                  pl.BlockSpec((None, seq, width), lambda b, c, tab: (b, 0, 2 * ncol + c)),
                  _const_spec((2 * t, t))],
        out_specs=pl.BlockSpec((None, seq, width), lambda b, c, tab: (b, 0, c)),
        scratch_shapes=[pltpu.VMEM((ATT_HEADS, seq, width), BF16),
                        pltpu.VMEM((nblk, ATT_HEADS * t, width), BF16),
                        pltpu.VMEM((ATT_STAGES, ATT_HEADS, t, t), F32),
                        pltpu.VMEM((ATT_STAGES, ATT_HEADS, t, 2 * t), BF16),
                        pltpu.VMEM((ATT_STAGES, t, ATT_HEADS * t), BF16),
                        pltpu.VMEM((3, t, t), F32),
                        pltpu.VMEM((seq, width), F32)])
    return pl.pallas_call(
        functools.partial(_attn_kernel, heads=ATT_HEADS, steps=steps),
        grid_spec=grid_spec,
        out_shape=jax.ShapeDtypeStruct((bsz, seq, sbw), BF16),
        compiler_params=_params("parallel", "parallel"),
        name="stickbreak_attn",
    )(jnp.asarray(table), qkv3, qkv3, qkv3, tri)


def _s5_tables(lam_re, lam_im, log_dt, b_re, b_im, c_re, c_im):
    g, n = lam_re.shape
    p = b_re.shape[-1]
    gh = g // 2
    npair = gh // 2
    lr = jnp.minimum(lam_re.astype(F32), -1e-4)
    li = lam_im.astype(F32)
    dt = jnp.exp(log_dt.astype(F32))[:, None]
    mag = jnp.exp(lr * dt)
    lb_re = mag * jnp.cos(li * dt)
    lb_im = mag * jnp.sin(li * dt)
    den = lr * lr + li * li
    nr, ni = lb_re - 1.0, lb_im
    cf_re = (nr * lr + ni * li) / den
    cf_im = (ni * lr - nr * li) / den
    bb_re = cf_re[..., None] * b_re - cf_im[..., None] * b_im
    bb_im = cf_re[..., None] * b_im + cf_im[..., None] * b_re
    eye_par = jnp.eye(2, dtype=F32)
    eye_pair = jnp.eye(npair, dtype=F32)

    tb = jnp.stack([bb_re, bb_im], 0).transpose(1, 3, 0, 2)
    tb = tb.reshape(2, npair, 2, p, 2, n)
    tb = tb[..., None, :] * eye_par[None, None, :, None, None, :, None]
    tb = tb.reshape(2, npair, 2 * p, 4 * n)
    bm = (tb[:, :, :, None, :] * eye_pair[None, :, None, :, None]).reshape(2, gh * p, gh * 2 * n)

    tc = jnp.stack([c_re, -c_im], 0).astype(F32)
    tc = tc.reshape(2, 2, npair, 2, p, n).transpose(1, 2, 0, 3, 5, 4)
    tc = tc[..., None, :] * eye_par[None, None, None, :, None, :, None]
    tc = tc.reshape(2, npair, 4 * n, 2 * p)
    cm = (tc[:, :, :, None, :] * eye_pair[None, :, None, :, None]).reshape(2, gh * 2 * n, gh * p)

    def cols(v):
        v = v.reshape(2, npair, 1, 2, n)
        return jnp.broadcast_to(v, (2, npair, 2, 2, n)).reshape(1, -1)

    coef = jnp.concatenate([jnp.broadcast_to(cols(lb_re), (SUBLANES, 2 * g * n)),
                            jnp.broadcast_to(cols(lb_im), (SUBLANES, 2 * g * n))], 0)
    return bm.astype(BF16), cm.astype(BF16), coef


def _gelu_tanh(x):
    return 0.5 * x * (1.0 + jnp.tanh(math.sqrt(2.0 / math.pi) * (x + 0.044715 * (x * x * x))))


def _s5_kernel(u_ref, bm_ref, cm_ref, coef_ref, d_ref, wg_ref, bg_ref, o_ref,
               utb_ref, bu_ref, st_ref, h_ref, otb_ref, *, nb, tt, width):
    nslab = width // LANES
    ncols = st_ref.shape[1]
    half = ncols // 2
    tiles = nb // SUBLANES

    @pl.when(pl.program_id(0) == 0)
    def _():
        h_ref[...] = jnp.zeros_like(h_ref)

    for b in range(nb):
        for s in range(nslab):
            utb_ref[s, pl.ds(b, tt, stride=nb), :] = u_ref[b, :, s * LANES:(s + 1) * LANES]
    u_tb = jnp.concatenate([utb_ref[s] for s in range(nslab)], axis=1)

    hw = width // 2
    for hf in range(2):
        bu_ref[:, hf * half:(hf + 1) * half] = _bdot(u_tb[:, hf * hw:(hf + 1) * hw], bm_ref[hf])

    pairs = 4
    for cg in range(ncols // (2 * LANES * pairs)):
        base = cg * 2 * LANES * pairs
        re_c = [base + q * 2 * LANES for q in range(pairs)]
        im_c = [c + LANES for c in re_c]
        a_re = [coef_ref[0:SUBLANES, c:c + LANES] for c in re_c]
        a_im = [coef_ref[SUBLANES:2 * SUBLANES, c:c + LANES] for c in re_c]
        h0 = tuple(h_ref[s * SUBLANES:(s + 1) * SUBLANES, c:c + LANES]
                   for q in range(pairs) for c in (re_c[q], im_c[q]) for s in range(tiles))

        def step(t, h, re_c=re_c, im_c=im_c, a_re=a_re, a_im=a_im):
            res = [None] * (2 * pairs * tiles)
            r0 = pl.multiple_of(t * nb, nb)
            for q in range(pairs):
                for s in range(tiles):
                    rs = pl.ds(r0 + s * SUBLANES, SUBLANES)
                    hr = h[(2 * q) * tiles + s]
                    hi = h[(2 * q + 1) * tiles + s]
                    nr = a_re[q] * hr - a_im[q] * hi + bu_ref[rs, re_c[q]:re_c[q] + LANES]
                    ni = a_re[q] * hi + a_im[q] * hr + bu_ref[rs, im_c[q]:im_c[q] + LANES]
                    st_ref[rs, re_c[q]:re_c[q] + LANES] = nr
                    st_ref[rs, im_c[q]:im_c[q] + LANES] = ni
                    res[(2 * q) * tiles + s] = nr
                    res[(2 * q + 1) * tiles + s] = ni
            return tuple(res)

        hl = lax.fori_loop(0, tt, step, h0, unroll=4)
        k = 0
        for q in range(pairs):
            for c in (re_c[q], im_c[q]):
                for s in range(tiles):
                    h_ref[s * SUBLANES:(s + 1) * SUBLANES, c:c + LANES] = hl[k]
                    k += 1

    y = jnp.concatenate(
        [_bdot(st_ref[:, hf * half:(hf + 1) * half], cm_ref[hf]) for hf in range(2)], axis=1)
    y = _gelu_tanh(y + d_ref[...] * u_tb)
    hg = _bdot(y, wg_ref[...]) + bg_ref[...]
    out = hg[:, :width] * jax.nn.sigmoid(hg[:, width:])
    for s in range(nslab):
        otb_ref[s] = out[:, s * LANES:(s + 1) * LANES]
    for b in range(nb):
        o_ref[b] = jnp.concatenate(
            [otb_ref[s, pl.ds(b, tt, stride=nb), :] for s in range(nslab)], axis=1).astype(BF16)


def _s5(u3, bm, cm, coef, d, wg, bg):
    nb, seq, width = u3.shape
    tt = S5_TIME
    ncols = coef.shape[1]
    kern = functools.partial(_s5_kernel, nb=nb, tt=tt, width=width)
    return pl.pallas_call(
        kern,
        grid=(seq // tt,),
        in_specs=[pl.BlockSpec((nb, tt, width), lambda i: (0, i, 0)),
                  _const_spec(bm.shape), _const_spec(cm.shape), _const_spec(coef.shape),
                  _const_spec(d.shape), _const_spec(wg.shape), _const_spec(bg.shape)],
        out_specs=pl.BlockSpec((nb, tt, width), lambda i: (0, i, 0)),
        out_shape=jax.ShapeDtypeStruct((nb, seq, width), BF16),
        scratch_shapes=[pltpu.VMEM((width // LANES, tt * nb, LANES), F32),
                        pltpu.VMEM((tt * nb, ncols), F32),
                        pltpu.VMEM((tt * nb, ncols), F32),
                        pltpu.VMEM((nb, ncols), F32),
                        pltpu.VMEM((width // LANES, tt * nb, LANES), F32)],
        compiler_params=_params("arbitrary"),
        name="s5_mixer",
    )(u3, bm, cm, coef, d, wg, bg)


def _ffn_tail(x, w1_ref, b1_ref, w2_ref, b2_ref, g_ref, be_ref):
    xb = x.astype(BF16)
    acc = ALPHA * x + b2_ref[...]
    for c in range(w1_ref.shape[1] // FFN_CHUNK):
        cs = slice(c * FFN_CHUNK, (c + 1) * FFN_CHUNK)
        h = jnp.dot(xb, w1_ref[:, cs], preferred_element_type=F32) + b1_ref[:, cs]
        h = jnp.square(jnp.maximum(h, 0.0))
        acc = acc + jnp.dot(h.astype(BF16), w2_ref[cs, :], preferred_element_type=F32)
    return _layer_norm(acc, g_ref[...], be_ref[...])


def _ffn_kernel(x_ref, w1_ref, b1_ref, w2_ref, b2_ref, g_ref, be_ref, o_ref):
    o_ref[...] = _ffn_tail(x_ref[...], w1_ref, b1_ref, w2_ref, b2_ref, g_ref, be_ref)


def _mix_ffn_kernel(x_ref, a_ref, s_ref, wo_ref, bo_ref, g1_ref, be1_ref,
                    w1_ref, b1_ref, w2_ref, b2_ref, g2_ref, be2_ref, o_ref, *, s5w):
    mix = (jnp.dot(a_ref[...], wo_ref[:s5w, :], preferred_element_type=F32)
           + jnp.dot(s_ref[...], wo_ref[s5w:, :], preferred_element_type=F32) + bo_ref[...])
    x1 = _layer_norm(ALPHA * x_ref[...] + mix, g1_ref[...], be1_ref[...])
    o_ref[...] = _ffn_tail(x1, w1_ref, b1_ref, w2_ref, b2_ref, g2_ref, be2_ref)


def _row_spec(width):
    return pl.BlockSpec((ROW_TILE, width), lambda i: (i, 0))


def _ffn(x2, w1, b1, w2, b2, g, be):
    n, d = x2.shape
    consts = (w1, b1, w2, b2, g, be)
    return pl.pallas_call(
        _ffn_kernel,
        grid=(n // ROW_TILE,),
        in_specs=[_row_spec(d)] + [_const_spec(c.shape) for c in consts],
        out_specs=_row_spec(d),
        out_shape=jax.ShapeDtypeStruct((n, d), F32),
        compiler_params=_params("parallel"),
        name="ffn_ln",
    )(x2, *consts)


def _mix_ffn(x2, s5o, sbo, wo, bo, g1, be1, w1, b1, w2, b2, g2, be2):
    n, d = x2.shape
    s5w, sbw = s5o.shape[1], sbo.shape[1]
    consts = (wo, bo, g1, be1, w1, b1, w2, b2, g2, be2)
    return pl.pallas_call(
        functools.partial(_mix_ffn_kernel, s5w=s5w),
        grid=(n // ROW_TILE,),
        in_specs=[_row_spec(d), _row_spec(s5w), _row_spec(sbw)]
                 + [_const_spec(c.shape) for c in consts],
        out_specs=_row_spec(d),
        out_shape=jax.ShapeDtypeStruct((n, d), F32),
        compiler_params=_params("parallel"),
        name="out_proj_ffn_ln",
    )(x2, s5o, sbo, *consts)


def _conv_kernel(x_ref, w1_ref, b1_ref, wd_ref, bd_ref, cg_ref, cb_ref, w2_ref, b2_ref,
                 g_ref, be_ref, o_ref, buf_ref, cv_ref, *, width):
    t = CONV_TILE
    nslab = width // LANES

    @pl.when(pl.program_id(1) == 0)
    def _():
        buf_ref[:, 0:CONV_HIST, :] = jnp.zeros((nslab, CONV_HIST, LANES), F32)

    x = x_ref[...]
    h = _bdot(x, w1_ref[...]) + b1_ref[...]
    glu = h[:, :width] * jax.nn.sigmoid(h[:, width:])
    for cb in range(nslab):
        buf_ref[cb, CONV_HIST:CONV_HIST + t, :] = glu[:, cb * LANES:(cb + 1) * LANES]

    first = CONV_HIST - (CONV_SIZE - 1)
    for cb in range(nslab):
        cs = slice(cb * LANES, (cb + 1) * LANES)
        for par in range(2):
            acc = jnp.broadcast_to(bd_ref[:, cs], (t // 2, LANES))
            for k in range(CONV_SIZE):
                acc = acc + wd_ref[k:k + 1, cs] * buf_ref[cb, pl.ds(first + k + par, t // 2, stride=2), :]
            cv_ref[cb, pl.ds(par, t // 2, stride=2), :] = acc
        buf_ref[cb, 0:CONV_HIST, :] = buf_ref[cb, t:t + CONV_HIST, :]

    conv = jnp.concatenate([cv_ref[cb] for cb in range(nslab)], axis=1)
    c = _layer_norm(conv, cg_ref[...], cb_ref[...])
    c = c * jax.nn.sigmoid(c)
    mix = _bdot(c, w2_ref[...]) + b2_ref[...]
    o_ref[...] = _layer_norm(ALPHA * x + mix, g_ref[...], be_ref[...])


def _conv_layer(x3, w1, b1, wd, bd, cg, cb, w2, b2, g, be):
    bsz, seq, d = x3.shape
    width = w2.shape[0]
    t = CONV_TILE
    blk = pl.BlockSpec((None, t, d), lambda b, i: (b, i, 0))
    consts = (w1, b1, wd, bd, cg, cb, w2, b2, g, be)
    return pl.pallas_call(
        functools.partial(_conv_kernel, width=width),
        grid=(bsz, seq // t),
        in_specs=[blk] + [_const_spec(c.shape) for c in consts],
        out_specs=blk,
        out_shape=jax.ShapeDtypeStruct((bsz, seq, d), F32),
        scratch_shapes=[pltpu.VMEM((width // LANES, CONV_HIST + t, LANES), F32),
                        pltpu.VMEM((width // LANES, t, LANES), F32)],
        compiler_params=_params("parallel", "arbitrary"),
        name="conformer_conv_ln",
    )(x3, *consts)


def _row(v):
    return v.astype(F32).reshape(1, -1)


def kernel(x, ln1_g, ln1_b, ln2_g, ln2_b, ffn_w1, ffn_b1, ffn_w2, ffn_b2, mix_w_in, mix_b_in, s5_lambda_re, s5_lambda_im, s5_log_dt, s5_b_re, s5_b_im, s5_c_re, s5_c_im, s5_d, s5_w_glu, s5_b_glu, mix_w_out, mix_b_out, conv_w_pw1, conv_b_pw1, conv_w_dw, conv_b_dw, conv_ln_g, conv_ln_b, conv_w_pw2, conv_b_pw2):
    bsz, seq, d = x.shape
    depth = ln1_g.shape[0]
    s5w = s5_d.shape[1]
    sbw = (mix_w_in.shape[2] - s5w) // 3
    n = bsz * seq
    x2 = x.reshape(n, d)
    for layer in range(depth):
        i = layer // 2
        ffn = (ffn_w1[layer].astype(BF16), _row(ffn_b1[layer]),
               ffn_w2[layer].astype(BF16), _row(ffn_b2[layer]),
               _row(ln2_g[layer]), _row(ln2_b[layer]))
        if layer % 2 == 0:
            u, qkv = _in_proj(x2, mix_w_in[i].astype(BF16), _row(mix_b_in[i]), s5w, sbw)
            sbo = _attention(qkv.reshape(bsz, seq, 3 * sbw), sbw)
            bm, cm, coef = _s5_tables(s5_lambda_re[i], s5_lambda_im[i], s5_log_dt[i],
                                      s5_b_re[i], s5_b_im[i], s5_c_re[i], s5_c_im[i])
            s5o = _s5(u.reshape(bsz, seq, s5w), bm, cm, coef, _row(s5_d[i]),
                      s5_w_glu[i].astype(BF16), _row(s5_b_glu[i]))
            x2 = _mix_ffn(x2, s5o.reshape(n, s5w), sbo.reshape(n, sbw),
                          mix_w_out[i].astype(BF16), _row(mix_b_out[i]),
                          _row(ln1_g[layer]), _row(ln1_b[layer]), *ffn)
        else:
            x1 = _conv_layer(x2.reshape(bsz, seq, d), conv_w_pw1[i].astype(BF16),
                             _row(conv_b_pw1[i]), conv_w_dw[i].astype(F32), _row(conv_b_dw[i]),
                             _row(conv_ln_g[i]), _row(conv_ln_b[i]),
                             conv_w_pw2[i].astype(BF16), _row(conv_b_pw2[i]),
                             _row(ln1_g[layer]), _row(ln1_b[layer]))
            x2 = _ffn(x1.reshape(n, d), *ffn)
    return x2.reshape(bsz, seq, d)
```

```python
import functools
import math

import numpy as np
import jax
import jax.numpy as jnp
from jax import lax
from jax.experimental import pallas as pl
from jax.experimental.pallas import tpu as pltpu

F32 = jnp.float32
BF16 = jnp.bfloat16

LANES = 128
SUBLANES = 8
VMEM_LIMIT = 56 * 1024 * 1024

S5_GROUP = 16
S5_STATE = 64
SB_HEAD_DIM = 64
CONV_SIZE = 31
LN_EPS = 1e-5
DEPTH = 2
ALPHA = (2 * DEPTH) ** 0.25

ROW_TILE = 512
FFN_CHUNK = 1024
ATT_TILE = 256
ATT_HEADS = 4
ATT_STAGES = 4
MASK_BIAS = -1e30
EXP2_CLAMP = 126.0
S5_TIME = 64
CONV_TILE = 256
CONV_HIST = 32


def _params(*sem):
    return pltpu.CompilerParams(dimension_semantics=sem, vmem_limit_bytes=VMEM_LIMIT)


def _const_spec(shape):
    nd = len(shape)
    return pl.BlockSpec(shape, lambda *_: (0,) * nd, pipeline_mode=pl.Buffered(1))


def _layer_norm(r, g, b):
    mu = jnp.mean(r, axis=-1, keepdims=True)
    c = r - mu
    var = jnp.mean(c * c, axis=-1, keepdims=True)
    return c * lax.rsqrt(var + LN_EPS) * g + b


def _bdot(a, w):
    return jnp.dot(a.astype(BF16), w, preferred_element_type=F32)


_sigmoid = jax.nn.sigmoid


def _in_proj_kernel(x_ref, w_ref, b_ref, u_ref, qkv_ref, *, s5w, sbw, scale):
    h = _bdot(x_ref[...], w_ref[...]) + b_ref[...]
    u_ref[...] = h[:, :s5w]
    qkv_ref[:, :sbw] = (h[:, s5w:s5w + sbw] * scale).astype(BF16)
    qkv_ref[:, sbw:] = h[:, s5w + sbw:].astype(BF16)


def _in_proj(x2, w, b, s5w, sbw):
    n, d = x2.shape
    wout = w.shape[1]
    kern = functools.partial(_in_proj_kernel, s5w=s5w, sbw=sbw,
                             scale=SB_HEAD_DIM ** -0.5 * math.log2(math.e))
    return pl.pallas_call(
        kern,
        grid=(n // ROW_TILE,),
        in_specs=[pl.BlockSpec((ROW_TILE, d), lambda i: (i, 0)),
                  _const_spec((d, wout)), _const_spec((1, wout))],
        out_specs=[pl.BlockSpec((ROW_TILE, s5w), lambda i: (i, 0)),
                   pl.BlockSpec((ROW_TILE, 3 * sbw), lambda i: (i, 0))],
        out_shape=[jax.ShapeDtypeStruct((n, s5w), F32),
                   jax.ShapeDtypeStruct((n, 3 * sbw), BF16)],
        compiler_params=_params("parallel"),
        name="in_proj",
    )(x2, w, b)


def _attn_tables(nblk):
    lead = ATT_STAGES - 1
    tiles = [(0, 0, 2)] * lead
    for r in range(nblk):
        tiles += [(r, c, 1 if c == r else 0) for c in range(r, -1, -1)]
    steps = -(-len(tiles) // ATT_STAGES) * ATT_STAGES
    tiles += [(0, 0, 2)] * (steps + lead - len(tiles))
    return np.asarray(tiles, np.int32).T.copy(), steps


def _attn_kernel(tab_ref, q_ref, k_ref, v_ref, tri_ref, o_ref,
                 qm_ref, vm_ref, z_ref, s_ref, w_ref, bias_ref, acc_ref, *, heads, steps):
    t = ATT_TILE
    seq, width = q_ref.shape
    nblk = seq // t
    nt = (((1,), (1,)), ((), ()))
    lane = lax.broadcasted_iota(jnp.int32, (1, width), 1)
    masks = [(lane >= h * SB_HEAD_DIM) & (lane < (h + 1) * SB_HEAD_DIM) for h in range(heads)]
    zero = jnp.zeros((), BF16)
    for r in range(nblk):
        rows = slice(r * t, (r + 1) * t)
        qr = q_ref[rows, :]
        vr = v_ref[rows, :]
        for h in range(heads):
            qm_ref[h, rows, :] = jnp.where(masks[h], qr, zero)
            vm_ref[r, h * t:(h + 1) * t, :] = jnp.where(masks[h], vr, zero)
    @pl.when((pl.program_id(0) == 0) & (pl.program_id(1) == 0))
    def _():
        row = lax.broadcasted_iota(jnp.int32, (t, t), 0)
        col = lax.broadcasted_iota(jnp.int32, (t, t), 1)
        bias_ref[0] = jnp.zeros((t, t), F32)
        bias_ref[1] = jnp.where(col < row, 0.0, MASK_BIAS)
        bias_ref[2] = jnp.full((t, t), MASK_BIAS, F32)
        z_ref[...] = jnp.full(z_ref.shape, MASK_BIAS, F32)
        s_ref[...] = jnp.zeros(s_ref.shape, BF16)
        w_ref[...] = jnp.zeros(w_ref.shape, BF16)

    acc_ref[...] = jnp.zeros(acc_ref.shape, F32)
    tri2 = tri_ref[...]

    def substep(n, r, carries):
        slot = r % ATT_STAGES
        rows = pl.ds(pl.multiple_of(tab_ref[0, n] * t, t), t)
        acc_ref[rows, :] += jnp.dot(w_ref[slot], vm_ref[tab_ref[1, n]],
                                    preferred_element_type=F32)
        slot = (r + 1) % ATT_STAGES
        row_start = tab_ref[2, n + 1] == 1
        new_carries = []
        for h in range(heads):
            c = jnp.where(row_start, 0.0, carries[h])
            incl = jnp.dot(s_ref[slot, h], tri2, preferred_element_type=F32)
            w_ref[slot, :, h * t:(h + 1) * t] = jnp.exp2((z_ref[slot, h] - c) - incl).astype(BF16)
            new_carries.append(c + incl[:, 0:1])
        slot = (r + 2) % ATT_STAGES
        for h in range(heads):
            z = z_ref[slot, h]
            sp = jnp.maximum(z, jnp.log2(1.0 + jnp.exp2(jnp.minimum(z, EXP2_CLAMP))))
            hi = sp.astype(BF16)
            s_ref[slot, h, :, 0:t] = hi
            s_ref[slot, h, :, t:2 * t] = (sp - hi.astype(F32)).astype(BF16)
        slot = (r + 3) % ATT_STAGES
        kj = k_ref[pl.ds(pl.multiple_of(tab_ref[1, n + 3] * t, t), t), :]
        bias = bias_ref[tab_ref[2, n + 3]]
        qrows = pl.ds(pl.multiple_of(tab_ref[0, n + 3] * t, t), t)
        for h in range(heads):
            z_ref[slot, h] = lax.dot_general(qm_ref[h, qrows, :], kj, nt,
                                             preferred_element_type=F32) + bias
        return tuple(new_carries)

    def body(it, carries):
        for r in range(ATT_STAGES):
            carries = substep(ATT_STAGES * it + r, r, carries)
        return carries

    lax.fori_loop(0, steps // ATT_STAGES, body,
                  tuple(jnp.zeros((t, 1), F32) for _ in range(heads)))
    o_ref[...] = acc_ref[...].astype(BF16)


def _attention(qkv3, sbw):
    bsz, seq, _ = qkv3.shape
    width = ATT_HEADS * SB_HEAD_DIM
    ncol = sbw // width
    t = ATT_TILE
    nblk = seq // t
    table, steps = _attn_tables(nblk)
    tri = (lax.broadcasted_iota(jnp.int32, (2 * t, t), 0) % t
           >= lax.broadcasted_iota(jnp.int32, (2 * t, t), 1)).astype(BF16)
    grid_spec = pltpu.PrefetchScalarGridSpec(
        num_scalar_prefetch=1,
        grid=(bsz, ncol),
        in_specs=[pl.BlockSpec((None, seq, width), lambda b, c, tab: (b, 0, c)),
                  pl.BlockSpec((None, seq, width), lambda b, c, tab: (b, 0, ncol + c)),
                  pl.BlockSpec((None, seq, width), lambda b, c, tab: (b, 0, 2 * ncol + c)),
                  _const_spec((2 * t, t))],
        out_specs=pl.BlockSpec((None, seq, width), lambda b, c, tab: (b, 0, c)),
        scratch_shapes=[pltpu.VMEM((ATT_HEADS, seq, width), BF16),
                        pltpu.VMEM((nblk, ATT_HEADS * t, width), BF16),
                        pltpu.VMEM((ATT_STAGES, ATT_HEADS, t, t), F32),
                        pltpu.VMEM((ATT_STAGES, ATT_HEADS, t, 2 * t), BF16),
                        pltpu.VMEM((ATT_STAGES, t, ATT_HEADS * t), BF16),
                        pltpu.VMEM((3, t, t), F32),
                        pltpu.VMEM((seq, width), F32)])
    return pl.pallas_call(
        functools.partial(_attn_kernel, heads=ATT_HEADS, steps=steps),
        grid_spec=grid_spec,
        out_shape=jax.ShapeDtypeStruct((bsz, seq, sbw), BF16),
        compiler_params=_params("arbitrary", "arbitrary"),
        name="stickbreak_attn",
    )(jnp.asarray(table), qkv3, qkv3, qkv3, tri)


def _s5_tables(lam_re, lam_im, log_dt, b_re, b_im, c_re, c_im):
    g, n = lam_re.shape
    p = b_re.shape[-1]
    gh = g // 2
    npair = gh // 2
    lr = jnp.minimum(lam_re.astype(F32), -1e-4)
    li = lam_im.astype(F32)
    dt = jnp.exp(log_dt.astype(F32))[:, None]
    mag = jnp.exp(lr * dt)
    lb_re = mag * jnp.cos(li * dt)
    lb_im = mag * jnp.sin(li * dt)
    den = lr * lr + li * li
    nr, ni = lb_re - 1.0, lb_im
    cf_re = (nr * lr + ni * li) / den
    cf_im = (ni * lr - nr * li) / den
    bb_re = cf_re[..., None] * b_re - cf_im[..., None] * b_im
    bb_im = cf_re[..., None] * b_im + cf_im[..., None] * b_re
    eye_par = jnp.eye(2, dtype=F32)
    eye_pair = jnp.eye(npair, dtype=F32)

    tb = jnp.stack([bb_re, bb_im], 0).transpose(1, 3, 0, 2)
    tb = tb.reshape(2, npair, 2, p, 2, n)
    tb = tb[..., None, :] * eye_par[None, None, :, None, None, :, None]
    tb = tb.reshape(2, npair, 2 * p, 4 * n)
    bm = (tb[:, :, :, None, :] * eye_pair[None, :, None, :, None]).reshape(2, gh * p, gh * 2 * n)

    tc = jnp.stack([c_re, -c_im], 0).astype(F32)
    tc = tc.reshape(2, 2, npair, 2, p, n).transpose(1, 2, 0, 3, 5, 4)
    tc = tc[..., None, :] * eye_par[None, None, None, :, None, :, None]
    tc = tc.reshape(2, npair, 4 * n, 2 * p)
    cm = (tc[:, :, :, None, :] * eye_pair[None, :, None, :, None]).reshape(2, gh * 2 * n, gh * p)

    def cols(v):
        v = v.reshape(2, npair, 1, 2, n)
        return jnp.broadcast_to(v, (2, npair, 2, 2, n)).reshape(1, -1)

    coef = jnp.concatenate([jnp.broadcast_to(cols(lb_re), (SUBLANES, 2 * g * n)),
                            jnp.broadcast_to(cols(lb_im), (SUBLANES, 2 * g * n))], 0)
    return bm.astype(BF16), cm.astype(BF16), coef


def _gelu_tanh(x):
    return 0.5 * x * (1.0 + jnp.tanh(math.sqrt(2.0 / math.pi) * (x + 0.044715 * (x * x * x))))


def _s5_kernel(u_ref, bm_ref, cm_ref, coef_ref, d_ref, wg_ref, bg_ref, o_ref,
               utb_ref, bu_ref, st_ref, h_ref, otb_ref, *, nb, tt, width):
    nslab = width // LANES
    ncols = st_ref.shape[1]
    half = ncols // 2
    tiles = nb // SUBLANES

    @pl.when(pl.program_id(0) == 0)
    def _():
        h_ref[...] = jnp.zeros_like(h_ref)

    for b in range(nb):
        for s in range(nslab):
            utb_ref[s, pl.ds(b, tt, stride=nb), :] = u_ref[b, :, s * LANES:(s + 1) * LANES]
    u_tb = jnp.concatenate([utb_ref[s] for s in range(nslab)], axis=1)

    hw = width // 2
    for hf in range(2):
        bu_ref[:, hf * half:(hf + 1) * half] = _bdot(u_tb[:, hf * hw:(hf + 1) * hw], bm_ref[hf])

    pairs = 4
    for cg in range(ncols // (2 * LANES * pairs)):
        base = cg * 2 * LANES * pairs
        re_c = [base + q * 2 * LANES for q in range(pairs)]
        im_c = [c + LANES for c in re_c]
        a_re = [coef_ref[0:SUBLANES, c:c + LANES] for c in re_c]
        a_im = [coef_ref[SUBLANES:2 * SUBLANES, c:c + LANES] for c in re_c]
        h0 = tuple(h_ref[s * SUBLANES:(s + 1) * SUBLANES, c:c + LANES]
                   for q in range(pairs) for c in (re_c[q], im_c[q]) for s in range(tiles))

        def step(t, h, re_c=re_c, im_c=im_c, a_re=a_re, a_im=a_im):
            res = [None] * (2 * pairs * tiles)
            r0 = pl.multiple_of(t * nb, nb)
            for q in range(pairs):
                for s in range(tiles):
                    rs = pl.ds(r0 + s * SUBLANES, SUBLANES)
                    hr = h[(2 * q) * tiles + s]
                    hi = h[(2 * q + 1) * tiles + s]
                    nr = a_re[q] * hr - a_im[q] * hi + bu_ref[rs, re_c[q]:re_c[q] + LANES]
                    ni = a_re[q] * hi + a_im[q] * hr + bu_ref[rs, im_c[q]:im_c[q] + LANES]
                    st_ref[rs, re_c[q]:re_c[q] + LANES] = nr
                    st_ref[rs, im_c[q]:im_c[q] + LANES] = ni
                    res[(2 * q) * tiles + s] = nr
                    res[(2 * q + 1) * tiles + s] = ni
            return tuple(res)

        hl = lax.fori_loop(0, tt, step, h0, unroll=4)
        k = 0
        for q in range(pairs):
            for c in (re_c[q], im_c[q]):
                for s in range(tiles):
                    h_ref[s * SUBLANES:(s + 1) * SUBLANES, c:c + LANES] = hl[k]
                    k += 1

    y = jnp.concatenate(
        [_bdot(st_ref[:, hf * half:(hf + 1) * half], cm_ref[hf]) for hf in range(2)], axis=1)
    y = _gelu_tanh(y + d_ref[...] * u_tb)
    hg = _bdot(y, wg_ref[...]) + bg_ref[...]
    out = hg[:, :width] * _sigmoid(hg[:, width:])
    for s in range(nslab):
        otb_ref[s] = out[:, s * LANES:(s + 1) * LANES]
    for b in range(nb):
        o_ref[b] = jnp.concatenate(
            [otb_ref[s, pl.ds(b, tt, stride=nb), :] for s in range(nslab)], axis=1).astype(BF16)


def _s5(u3, bm, cm, coef, d, wg, bg):
    nb, seq, width = u3.shape
    tt = S5_TIME
    ncols = coef.shape[1]
    kern = functools.partial(_s5_kernel, nb=nb, tt=tt, width=width)
    return pl.pallas_call(
        kern,
        grid=(seq // tt,),
        in_specs=[pl.BlockSpec((nb, tt, width), lambda i: (0, i, 0)),
                  _const_spec(bm.shape), _const_spec(cm.shape), _const_spec(coef.shape),
                  _const_spec(d.shape), _const_spec(wg.shape), _const_spec(bg.shape)],
        out_specs=pl.BlockSpec((nb, tt, width), lambda i: (0, i, 0)),
        out_shape=jax.ShapeDtypeStruct((nb, seq, width), BF16),
        scratch_shapes=[pltpu.VMEM((width // LANES, tt * nb, LANES), F32),
                        pltpu.VMEM((tt * nb, ncols), F32),
                        pltpu.VMEM((tt * nb, ncols), F32),
                        pltpu.VMEM((nb, ncols), F32),
                        pltpu.VMEM((width // LANES, tt * nb, LANES), F32)],
        compiler_params=_params("arbitrary"),
        name="s5_mixer",
    )(u3, bm, cm, coef, d, wg, bg)


def _ffn_tail(x, w1_ref, b1_ref, w2_ref, b2_ref, g_ref, be_ref):
    xb = x.astype(BF16)
    acc = ALPHA * x + b2_ref[...]
    for c in range(w1_ref.shape[1] // FFN_CHUNK):
        cs = slice(c * FFN_CHUNK, (c + 1) * FFN_CHUNK)
        h = jnp.dot(xb, w1_ref[:, cs], preferred_element_type=F32) + b1_ref[:, cs]
        h = jnp.square(jnp.maximum(h, 0.0))
        acc = acc + jnp.dot(h.astype(BF16), w2_ref[cs, :], preferred_element_type=F32)
    return _layer_norm(acc, g_ref[...], be_ref[...])


def _ffn_kernel(x_ref, w1_ref, b1_ref, w2_ref, b2_ref, g_ref, be_ref, o_ref):
    o_ref[...] = _ffn_tail(x_ref[...], w1_ref, b1_ref, w2_ref, b2_ref, g_ref, be_ref)


def _mix_ffn_kernel(x_ref, a_ref, s_ref, wo_ref, bo_ref, g1_ref, be1_ref,
                    w1_ref, b1_ref, w2_ref, b2_ref, g2_ref, be2_ref, o_ref, *, s5w):
    mix = (jnp.dot(a_ref[...], wo_ref[:s5w, :], preferred_element_type=F32)
           + jnp.dot(s_ref[...], wo_ref[s5w:, :], preferred_element_type=F32) + bo_ref[...])
    x1 = _layer_norm(ALPHA * x_ref[...] + mix, g1_ref[...], be1_ref[...])
    o_ref[...] = _ffn_tail(x1, w1_ref, b1_ref, w2_ref, b2_ref, g2_ref, be2_ref)


def _row_spec(width):
    return pl.BlockSpec((ROW_TILE, width), lambda i: (i, 0))


def _ffn(x2, w1, b1, w2, b2, g, be):
    n, d = x2.shape
    consts = (w1, b1, w2, b2, g, be)
    return pl.pallas_call(
        _ffn_kernel,
        grid=(n // ROW_TILE,),
        in_specs=[_row_spec(d)] + [_const_spec(c.shape) for c in consts],
        out_specs=_row_spec(d),
        out_shape=jax.ShapeDtypeStruct((n, d), F32),
        compiler_params=_params("parallel"),
        name="ffn_ln",
    )(x2, *consts)


def _mix_ffn(x2, s5o, sbo, wo, bo, g1, be1, w1, b1, w2, b2, g2, be2):
    n, d = x2.shape
    s5w, sbw = s5o.shape[1], sbo.shape[1]
    consts = (wo, bo, g1, be1, w1, b1, w2, b2, g2, be2)
    return pl.pallas_call(
        functools.partial(_mix_ffn_kernel, s5w=s5w),
        grid=(n // ROW_TILE,),
        in_specs=[_row_spec(d), _row_spec(s5w), _row_spec(sbw)]
                 + [_const_spec(c.shape) for c in consts],
        out_specs=_row_spec(d),
        out_shape=jax.ShapeDtypeStruct((n, d), F32),
        compiler_params=_params("parallel"),
        name="out_proj_ffn_ln",
    )(x2, s5o, sbo, *consts)


def _conv_kernel(x_ref, w1_ref, b1_ref, wd_ref, bd_ref, cg_ref, cb_ref, w2_ref, b2_ref,
                 g_ref, be_ref, o_ref, buf_ref, cv_ref, *, width):
    t = CONV_TILE
    nslab = width // LANES

    @pl.when(pl.program_id(1) == 0)
    def _():
        buf_ref[:, 0:CONV_HIST, :] = jnp.zeros((nslab, CONV_HIST, LANES), F32)

    x = x_ref[...]
    h = _bdot(x, w1_ref[...]) + b1_ref[...]
    glu = h[:, :width] * _sigmoid(h[:, width:])
    for cb in range(nslab):
        buf_ref[cb, CONV_HIST:CONV_HIST + t, :] = glu[:, cb * LANES:(cb + 1) * LANES]

    first = CONV_HIST - (CONV_SIZE - 1)
    for cb in range(nslab):
        cs = slice(cb * LANES, (cb + 1) * LANES)
        for par in range(2):
            acc = jnp.broadcast_to(bd_ref[:, cs], (t // 2, LANES))
            for k in range(CONV_SIZE):
                acc = acc + wd_ref[k:k + 1, cs] * buf_ref[cb, pl.ds(first + k + par, t // 2, stride=2), :]
            cv_ref[cb, pl.ds(par, t // 2, stride=2), :] = acc
        buf_ref[cb, 0:CONV_HIST, :] = buf_ref[cb, t:t + CONV_HIST, :]

    conv = jnp.concatenate([cv_ref[cb] for cb in range(nslab)], axis=1)
    c = _layer_norm(conv, cg_ref[...], cb_ref[...])
    c = c * _sigmoid(c)
    mix = _bdot(c, w2_ref[...]) + b2_ref[...]
    o_ref[...] = _layer_norm(ALPHA * x + mix, g_ref[...], be_ref[...])


def _conv_layer(x3, w1, b1, wd, bd, cg, cb, w2, b2, g, be):
    bsz, seq, d = x3.shape
    width = w2.shape[0]
    t = CONV_TILE
    blk = pl.BlockSpec((None, t, d), lambda b, i: (b, i, 0))
    consts = (w1, b1, wd, bd, cg, cb, w2, b2, g, be)
    return pl.pallas_call(
        functools.partial(_conv_kernel, width=width),
        grid=(bsz, seq // t),
        in_specs=[blk] + [_const_spec(c.shape) for c in consts],
        out_specs=blk,
        out_shape=jax.ShapeDtypeStruct((bsz, seq, d), F32),
        scratch_shapes=[pltpu.VMEM((width // LANES, CONV_HIST + t, LANES), F32),
                        pltpu.VMEM((width // LANES, t, LANES), F32)],
        compiler_params=_params("parallel", "arbitrary"),
        name="conformer_conv_ln",
    )(x3, *consts)


def _row(v):
    return v.astype(F32).reshape(1, -1)


def kernel(x, ln1_g, ln1_b, ln2_g, ln2_b, ffn_w1, ffn_b1, ffn_w2, ffn_b2, mix_w_in, mix_b_in, s5_lambda_re, s5_lambda_im, s5_log_dt, s5_b_re, s5_b_im, s5_c_re, s5_c_im, s5_d, s5_w_glu, s5_b_glu, mix_w_out, mix_b_out, conv_w_pw1, conv_b_pw1, conv_w_dw, conv_b_dw, conv_ln_g, conv_ln_b, conv_w_pw2, conv_b_pw2):
    bsz, seq, d = x.shape
    depth = ln1_g.shape[0]
    s5w = s5_d.shape[1]
    sbw = (mix_w_in.shape[2] - s5w) // 3
    n = bsz * seq
    x2 = x.reshape(n, d)
    for layer in range(depth):
        i = layer // 2
        ffn = (ffn_w1[layer].astype(BF16), _row(ffn_b1[layer]),
               ffn_w2[layer].astype(BF16), _row(ffn_b2[layer]),
               _row(ln2_g[layer]), _row(ln2_b[layer]))
        if layer % 2 == 0:
            u, qkv = _in_proj(x2, mix_w_in[i].astype(BF16), _row(mix_b_in[i]), s5w, sbw)
            sbo = _attention(qkv.reshape(bsz, seq, 3 * sbw), sbw)
            bm, cm, coef = _s5_tables(s5_lambda_re[i], s5_lambda_im[i], s5_log_dt[i],
                                      s5_b_re[i], s5_b_im[i], s5_c_re[i], s5_c_im[i])
            s5o = _s5(u.reshape(bsz, seq, s5w), bm, cm, coef, _row(s5_d[i]),
                      s5_w_glu[i].astype(BF16), _row(s5_b_glu[i]))
            x2 = _mix_ffn(x2, s5o.reshape(n, s5w), sbo.reshape(n, sbw),
                          mix_w_out[i].astype(BF16), _row(mix_b_out[i]),
                          _row(ln1_g[layer]), _row(ln1_b[layer]), *ffn)
        else:
            x1 = _conv_layer(x2.reshape(bsz, seq, d), conv_w_pw1[i].astype(BF16),
                             _row(conv_b_pw1[i]), conv_w_dw[i].astype(F32), _row(conv_b_dw[i]),
                             _row(conv_ln_g[i]), _row(conv_ln_b[i]),
                             conv_w_pw2[i].astype(BF16), _row(conv_b_pw2[i]),
                             _row(ln1_g[layer]), _row(ln1_b[layer]))
            x2 = _ffn(x1.reshape(n, d), *ffn)
    return x2.reshape(bsz, seq, d)
```

```python
import functools
import math

import numpy as np
import jax
import jax.numpy as jnp
from jax import lax
from jax.experimental import pallas as pl
from jax.experimental.pallas import tpu as pltpu

F32 = jnp.float32
BF16 = jnp.bfloat16

LANES = 128
SUBLANES = 8
VMEM_LIMIT = 56 * 1024 * 1024

S5_GROUP = 16
S5_STATE = 64
SB_HEAD_DIM = 64
CONV_SIZE = 31
LN_EPS = 1e-5
DEPTH = 2
ALPHA = (2 * DEPTH) ** 0.25

ROW_TILE = 512
FFN_CHUNK = 1024
ATT_TILE = 256
ATT_HEADS = 4
ATT_STAGES = 4
ATT_SPLIT = 1
MASK_BIAS = -1e30
EXP2_CLAMP = 126.0
S5_TIME = 64
CONV_TILE = 256
CONV_HIST = 32


def _params(*sem):
    return pltpu.CompilerParams(dimension_semantics=sem, vmem_limit_bytes=VMEM_LIMIT)


def _const_spec(shape):
    nd = len(shape)
    return pl.BlockSpec(shape, lambda *_: (0,) * nd, pipeline_mode=pl.Buffered(1))


def _layer_norm(r, g, b):
    mu = jnp.mean(r, axis=-1, keepdims=True)
    c = r - mu
    var = jnp.mean(c * c, axis=-1, keepdims=True)
    return c * lax.rsqrt(var + LN_EPS) * g + b


def _bdot(a, w):
    return jnp.dot(a.astype(BF16), w, preferred_element_type=F32)


_sigmoid = jax.nn.sigmoid


def _in_proj_kernel(x_ref, w_ref, b_ref, u_ref, qkv_ref, *, s5w, sbw, scale):
    h = _bdot(x_ref[...], w_ref[...]) + b_ref[...]
    u_ref[...] = h[:, :s5w]
    qkv_ref[:, :sbw] = (h[:, s5w:s5w + sbw] * scale).astype(BF16)
    qkv_ref[:, sbw:] = h[:, s5w + sbw:].astype(BF16)


def _in_proj(x2, w, b, s5w, sbw):
    n, d = x2.shape
    wout = w.shape[1]
    kern = functools.partial(_in_proj_kernel, s5w=s5w, sbw=sbw,
                             scale=SB_HEAD_DIM ** -0.5 * math.log2(math.e))
    return pl.pallas_call(
        kern,
        grid=(n // ROW_TILE,),
        in_specs=[pl.BlockSpec((ROW_TILE, d), lambda i: (i, 0)),
                  _const_spec((d, wout)), _const_spec((1, wout))],
        out_specs=[pl.BlockSpec((ROW_TILE, s5w), lambda i: (i, 0)),
                   pl.BlockSpec((ROW_TILE, 3 * sbw), lambda i: (i, 0))],
        out_shape=[jax.ShapeDtypeStruct((n, s5w), F32),
                   jax.ShapeDtypeStruct((n, 3 * sbw), BF16)],
        compiler_params=_params("parallel"),
        name="in_proj",
    )(x2, w, b)


def _attn_tables(nblk):
    lead = ATT_STAGES - 1
    tiles = [(0, 0, 2)] * lead
    for r in range(nblk):
        tiles += [(r, c, 1 if c == r else 0) for c in range(r, -1, -1)]
    steps = -(-len(tiles) // ATT_STAGES) * ATT_STAGES
    tiles += [(0, 0, 2)] * (steps + lead - len(tiles))
    return np.asarray(tiles, np.int32).T.copy(), steps


def _attn_kernel(tab_ref, q_ref, k_ref, v_ref, tri_ref, o_ref,
                 qm_ref, vm_ref, z_ref, s_ref, w_ref, bias_ref, acc_ref, *, heads, steps):
    t = ATT_TILE
    seq, width = q_ref.shape
    nblk = seq // t
    nt = (((1,), (1,)), ((), ()))
    lane = lax.broadcasted_iota(jnp.int32, (1, width), 1)
    masks = [(lane >= h * SB_HEAD_DIM) & (lane < (h + 1) * SB_HEAD_DIM) for h in range(heads)]
    zero = jnp.zeros((), BF16)
    for r in range(nblk):
        rows = slice(r * t, (r + 1) * t)
        qr = q_ref[rows, :]
        vr = v_ref[rows, :]
        for h in range(heads):
            qm_ref[h, rows, :] = jnp.where(masks[h], qr, zero)
            vm_ref[r, h * t:(h + 1) * t, :] = jnp.where(masks[h], vr, zero)
    @pl.when((pl.program_id(0) == 0) & (pl.program_id(1) == 0))
    def _():
        row = lax.broadcasted_iota(jnp.int32, (t, t), 0)
        col = lax.broadcasted_iota(jnp.int32, (t, t), 1)
        bias_ref[0] = jnp.zeros((t, t), F32)
        bias_ref[1] = jnp.where(col < row, 0.0, MASK_BIAS)
        bias_ref[2] = jnp.full((t, t), MASK_BIAS, F32)
        z_ref[...] = jnp.full(z_ref.shape, MASK_BIAS, F32)
        s_ref[...] = jnp.zeros(s_ref.shape, BF16)
        w_ref[...] = jnp.zeros(w_ref.shape, BF16)

    acc_ref[...] = jnp.zeros(acc_ref.shape, F32)
    tri2 = tri_ref[...]

    def substep(n, r, carries):
        slot = r % ATT_STAGES
        rows = pl.ds(pl.multiple_of(tab_ref[0, n] * t, t), t)
        acc_ref[rows, :] += jnp.dot(w_ref[slot], vm_ref[tab_ref[1, n]],
                                    preferred_element_type=F32)
        slot = (r + 1) % ATT_STAGES
        row_start = tab_ref[2, n + 1] == 1
        new_carries = []
        for h in range(heads):
            c = jnp.where(row_start, 0.0, carries[h])
            incl = jnp.dot(s_ref[slot, h], tri2, preferred_element_type=F32)
            w_ref[slot, :, h * t:(h + 1) * t] = jnp.exp2((z_ref[slot, h] - c) - incl).astype(BF16)
            new_carries.append(c + incl[:, 0:1])
        slot = (r + 2) % ATT_STAGES
        for h in range(heads):
            z = z_ref[slot, h]
            sp = jnp.maximum(z, jnp.log2(1.0 + jnp.exp2(jnp.minimum(z, EXP2_CLAMP))))
            for p in range(ATT_SPLIT):
                part = sp.astype(BF16)
                s_ref[slot, h, :, p * t:(p + 1) * t] = part
                if p + 1 < ATT_SPLIT:
                    sp = sp - part.astype(F32)
        slot = (r + 3) % ATT_STAGES
        kj = k_ref[pl.ds(pl.multiple_of(tab_ref[1, n + 3] * t, t), t), :]
        bias = bias_ref[tab_ref[2, n + 3]]
        qrows = pl.ds(pl.multiple_of(tab_ref[0, n + 3] * t, t), t)
        for h in range(heads):
            z_ref[slot, h] = lax.dot_general(qm_ref[h, qrows, :], kj, nt,
                                             preferred_element_type=F32) + bias
        return tuple(new_carries)

    def body(it, carries):
        for r in range(ATT_STAGES):
            carries = substep(ATT_STAGES * it + r, r, carries)
        return carries

    lax.fori_loop(0, steps // ATT_STAGES, body,
                  tuple(jnp.zeros((t, 1), F32) for _ in range(heads)))
    o_ref[...] = acc_ref[...].astype(BF16)


def _attention(qkv3, sbw):
    bsz, seq, _ = qkv3.shape
    width = ATT_HEADS * SB_HEAD_DIM
    ncol = sbw // width
    t = ATT_TILE
    nblk = seq // t
    table, steps = _attn_tables(nblk)
    tri = (lax.broadcasted_iota(jnp.int32, (ATT_SPLIT * t, t), 0) % t
           >= lax.broadcasted_iota(jnp.int32, (ATT_SPLIT * t, t), 1)).astype(BF16)
    grid_spec = pltpu.PrefetchScalarGridSpec(
        num_scalar_prefetch=1,
        grid=(bsz, ncol),
        in_specs=[pl.BlockSpec((None, seq, width), lambda b, c, tab: (b, 0, c)),
                  pl.BlockSpec((None, seq, width), lambda b, c, tab: (b, 0, ncol + c)),
                  pl.BlockSpec((None, seq, width), lambda b, c, tab: (b, 0, 2 * ncol + c)),
                  _const_spec((ATT_SPLIT * t, t))],
        out_specs=pl.BlockSpec((None, seq, width), lambda b, c, tab: (b, 0, c)),
        scratch_shapes=[pltpu.VMEM((ATT_HEADS, seq, width), BF16),
                        pltpu.VMEM((nblk, ATT_HEADS * t, width), BF16),
                        pltpu.VMEM((ATT_STAGES, ATT_HEADS, t, t), F32),
                        pltpu.VMEM((ATT_STAGES, ATT_HEADS, t, ATT_SPLIT * t), BF16),
                        pltpu.VMEM((ATT_STAGES, t, ATT_HEADS * t), BF16),
                        pltpu.VMEM((3, t, t), F32),
                        pltpu.VMEM((seq, width), F32)])
    return pl.pallas_call(
        functools.partial(_attn_kernel, heads=ATT_HEADS, steps=steps),
        grid_spec=grid_spec,
        out_shape=jax.ShapeDtypeStruct((bsz, seq, sbw), BF16),
        compiler_params=_params("arbitrary", "arbitrary"),
        name="stickbreak_attn",
    )(jnp.asarray(table), qkv3, qkv3, qkv3, tri)


def _s5_tables(lam_re, lam_im, log_dt, b_re, b_im, c_re, c_im):
    g, n = lam_re.shape
    p = b_re.shape[-1]
    gh = g // 2
    npair = gh // 2
    lr = jnp.minimum(lam_re.astype(F32), -1e-4)
    li = lam_im.astype(F32)
    dt = jnp.exp(log_dt.astype(F32))[:, None]
    mag = jnp.exp(lr * dt)
    lb_re = mag * jnp.cos(li * dt)
    lb_im = mag * jnp.sin(li * dt)
    den = lr * lr + li * li
    nr, ni = lb_re - 1.0, lb_im
    cf_re = (nr * lr + ni * li) / den
    cf_im = (ni * lr - nr * li) / den
    bb_re = cf_re[..., None] * b_re - cf_im[..., None] * b_im
    bb_im = cf_re[..., None] * b_im + cf_im[..., None] * b_re
    eye_par = jnp.eye(2, dtype=F32)
    eye_pair = jnp.eye(npair, dtype=F32)

    tb = jnp.stack([bb_re, bb_im], 0).transpose(1, 3, 0, 2)
    tb = tb.reshape(2, npair, 2, p, 2, n)
    tb = tb[..., None, :] * eye_par[None, None, :, None, None, :, None]
    tb = tb.reshape(2, npair, 2 * p, 4 * n)
    bm = (tb[:, :, :, None, :] * eye_pair[None, :, None, :, None]).reshape(2, gh * p, gh * 2 * n)

    tc = jnp.stack([c_re, -c_im], 0).astype(F32)
    tc = tc.reshape(2, 2, npair, 2, p, n).transpose(1, 2, 0, 3, 5, 4)
    tc = tc[..., None, :] * eye_par[None, None, None, :, None, :, None]
    tc = tc.reshape(2, npair, 4 * n, 2 * p)
    cm = (tc[:, :, :, None, :] * eye_pair[None, :, None, :, None]).reshape(2, gh * 2 * n, gh * p)

    def cols(v):
        v = v.reshape(2, npair, 1, 2, n)
        return jnp.broadcast_to(v, (2, npair, 2, 2, n)).reshape(1, -1)

    coef = jnp.concatenate([jnp.broadcast_to(cols(lb_re), (SUBLANES, 2 * g * n)),
                            jnp.broadcast_to(cols(lb_im), (SUBLANES, 2 * g * n))], 0)
    return bm.astype(BF16), cm.astype(BF16), coef


def _gelu_tanh(x):
    return 0.5 * x * (1.0 + jnp.tanh(math.sqrt(2.0 / math.pi) * (x + 0.044715 * (x * x * x))))


def _s5_kernel(u_ref, bm_ref, cm_ref, coef_ref, d_ref, wg_ref, bg_ref, o_ref,
               utb_ref, bu_ref, st_ref, h_ref, otb_ref, *, nb, tt, width):
    nslab = width // LANES
    ncols = st_ref.shape[1]
    half = ncols // 2
    tiles = nb // SUBLANES

    @pl.when(pl.program_id(0) == 0)
    def _():
        h_ref[...] = jnp.zeros_like(h_ref)

    for b in range(nb):
        for s in range(nslab):
            utb_ref[s, pl.ds(b, tt, stride=nb), :] = u_ref[b, :, s * LANES:(s + 1) * LANES]
    u_tb = jnp.concatenate([utb_ref[s] for s in range(nslab)], axis=1)

    hw = width // 2
    for hf in range(2):
        bu_ref[:, hf * half:(hf + 1) * half] = _bdot(u_tb[:, hf * hw:(hf + 1) * hw], bm_ref[hf])

    pairs = 4
    for cg in range(ncols // (2 * LANES * pairs)):
        base = cg * 2 * LANES * pairs
        re_c = [base + q * 2 * LANES for q in range(pairs)]
        im_c = [c + LANES for c in re_c]
        a_re = [coef_ref[0:SUBLANES, c:c + LANES] for c in re_c]
        a_im = [coef_ref[SUBLANES:2 * SUBLANES, c:c + LANES] for c in re_c]
        h0 = tuple(h_ref[s * SUBLANES:(s + 1) * SUBLANES, c:c + LANES]
                   for q in range(pairs) for c in (re_c[q], im_c[q]) for s in range(tiles))

        def step(t, h, re_c=re_c, im_c=im_c, a_re=a_re, a_im=a_im):
            res = [None] * (2 * pairs * tiles)
            r0 = pl.multiple_of(t * nb, nb)
            for q in range(pairs):
                for s in range(tiles):
                    rs = pl.ds(r0 + s * SUBLANES, SUBLANES)
                    hr = h[(2 * q) * tiles + s]
                    hi = h[(2 * q + 1) * tiles + s]
                    nr = a_re[q] * hr - a_im[q] * hi + bu_ref[rs, re_c[q]:re_c[q] + LANES]
                    ni = a_re[q] * hi + a_im[q] * hr + bu_ref[rs, im_c[q]:im_c[q] + LANES]
                    st_ref[rs, re_c[q]:re_c[q] + LANES] = nr
                    st_ref[rs, im_c[q]:im_c[q] + LANES] = ni
                    res[(2 * q) * tiles + s] = nr
                    res[(2 * q + 1) * tiles + s] = ni
            return tuple(res)

        hl = lax.fori_loop(0, tt, step, h0, unroll=4)
        k = 0
        for q in range(pairs):
            for c in (re_c[q], im_c[q]):
                for s in range(tiles):
                    h_ref[s * SUBLANES:(s + 1) * SUBLANES, c:c + LANES] = hl[k]
                    k += 1

    y = jnp.concatenate(
        [_bdot(st_ref[:, hf * half:(hf + 1) * half], cm_ref[hf]) for hf in range(2)], axis=1)
    y = _gelu_tanh(y + d_ref[...] * u_tb)
    hg = _bdot(y, wg_ref[...]) + bg_ref[...]
    out = hg[:, :width] * _sigmoid(hg[:, width:])
    for s in range(nslab):
        otb_ref[s] = out[:, s * LANES:(s + 1) * LANES]
    for b in range(nb):
        o_ref[b] = jnp.concatenate(
            [otb_ref[s, pl.ds(b, tt, stride=nb), :] for s in range(nslab)], axis=1).astype(BF16)


def _s5(u3, bm, cm, coef, d, wg, bg):
    nb, seq, width = u3.shape
    tt = S5_TIME
    ncols = coef.shape[1]
    kern = functools.partial(_s5_kernel, nb=nb, tt=tt, width=width)
    return pl.pallas_call(
        kern,
        grid=(seq // tt,),
        in_specs=[pl.BlockSpec((nb, tt, width), lambda i: (0, i, 0)),
                  _const_spec(bm.shape), _const_spec(cm.shape), _const_spec(coef.shape),
                  _const_spec(d.shape), _const_spec(wg.shape), _const_spec(bg.shape)],
        out_specs=pl.BlockSpec((nb, tt, width), lambda i: (0, i, 0)),
        out_shape=jax.ShapeDtypeStruct((nb, seq, width), BF16),
        scratch_shapes=[pltpu.VMEM((width // LANES, tt * nb, LANES), F32),
                        pltpu.VMEM((tt * nb, ncols), F32),
                        pltpu.VMEM((tt * nb, ncols), F32),
                        pltpu.VMEM((nb, ncols), F32),
                        pltpu.VMEM((width // LANES, tt * nb, LANES), F32)],
        compiler_params=_params("arbitrary"),
        name="s5_mixer",
    )(u3, bm, cm, coef, d, wg, bg)


def _ffn_tail(x, w1_ref, b1_ref, w2_ref, b2_ref, g_ref, be_ref):
    xb = x.astype(BF16)
    acc = ALPHA * x + b2_ref[...]
    for c in range(w1_ref.shape[1] // FFN_CHUNK):
        cs = slice(c * FFN_CHUNK, (c + 1) * FFN_CHUNK)
        h = jnp.dot(xb, w1_ref[:, cs], preferred_element_type=F32) + b1_ref[:, cs]
        h = jnp.square(jnp.maximum(h, 0.0))
        acc = acc + jnp.dot(h.astype(BF16), w2_ref[cs, :], preferred_element_type=F32)
    return _layer_norm(acc, g_ref[...], be_ref[...])


def _ffn_kernel(x_ref, w1_ref, b1_ref, w2_ref, b2_ref, g_ref, be_ref, o_ref):
    o_ref[...] = _ffn_tail(x_ref[...], w1_ref, b1_ref, w2_ref, b2_ref, g_ref, be_ref)


def _mix_ffn_kernel(x_ref, a_ref, s_ref, wo_ref, bo_ref, g1_ref, be1_ref,
                    w1_ref, b1_ref, w2_ref, b2_ref, g2_ref, be2_ref, o_ref, *, s5w):
    mix = (jnp.dot(a_ref[...], wo_ref[:s5w, :], preferred_element_type=F32)
           + jnp.dot(s_ref[...], wo_ref[s5w:, :], preferred_element_type=F32) + bo_ref[...])
    x1 = _layer_norm(ALPHA * x_ref[...] + mix, g1_ref[...], be1_ref[...])
    o_ref[...] = _ffn_tail(x1, w1_ref, b1_ref, w2_ref, b2_ref, g2_ref, be2_ref)


def _row_spec(width):
    return pl.BlockSpec((ROW_TILE, width), lambda i: (i, 0))


def _ffn(x2, w1, b1, w2, b2, g, be):
    n, d = x2.shape
    consts = (w1, b1, w2, b2, g, be)
    return pl.pallas_call(
        _ffn_kernel,
        grid=(n // ROW_TILE,),
        in_specs=[_row_spec(d)] + [_const_spec(c.shape) for c in consts],
        out_specs=_row_spec(d),
        out_shape=jax.ShapeDtypeStruct((n, d), F32),
        compiler_params=_params("parallel"),
        name="ffn_ln",
    )(x2, *consts)


def _mix_ffn(x2, s5o, sbo, wo, bo, g1, be1, w1, b1, w2, b2, g2, be2):
    n, d = x2.shape
    s5w, sbw = s5o.shape[1], sbo.shape[1]
    consts = (wo, bo, g1, be1, w1, b1, w2, b2, g2, be2)
    return pl.pallas_call(
        functools.partial(_mix_ffn_kernel, s5w=s5w),
        grid=(n // ROW_TILE,),
        in_specs=[_row_spec(d), _row_spec(s5w), _row_spec(sbw)]
                 + [_const_spec(c.shape) for c in consts],
        out_specs=_row_spec(d),
        out_shape=jax.ShapeDtypeStruct((n, d), F32),
        compiler_params=_params("parallel"),
        name="out_proj_ffn_ln",
    )(x2, s5o, sbo, *consts)


def _conv_kernel(x_ref, w1_ref, b1_ref, wd_ref, bd_ref, cg_ref, cb_ref, w2_ref, b2_ref,
                 g_ref, be_ref, o_ref, buf_ref, cv_ref, *, width):
    t = CONV_TILE
    nslab = width // LANES

    @pl.when(pl.program_id(1) == 0)
    def _():
        buf_ref[:, 0:CONV_HIST, :] = jnp.zeros((nslab, CONV_HIST, LANES), F32)

    x = x_ref[...]
    h = _bdot(x, w1_ref[...]) + b1_ref[...]
    glu = h[:, :width] * _sigmoid(h[:, width:])
    for cb in range(nslab):
        buf_ref[cb, CONV_HIST:CONV_HIST + t, :] = glu[:, cb * LANES:(cb + 1) * LANES]

    first = CONV_HIST - (CONV_SIZE - 1)
    for cb in range(nslab):
        cs = slice(cb * LANES, (cb + 1) * LANES)
        for par in range(2):
            acc = jnp.broadcast_to(bd_ref[:, cs], (t // 2, LANES))
            for k in range(CONV_SIZE):
                acc = acc + wd_ref[k:k + 1, cs] * buf_ref[cb, pl.ds(first + k + par, t // 2, stride=2), :]
            cv_ref[cb, pl.ds(par, t // 2, stride=2), :] = acc
        buf_ref[cb, 0:CONV_HIST, :] = buf_ref[cb, t:t + CONV_HIST, :]

    conv = jnp.concatenate([cv_ref[cb] for cb in range(nslab)], axis=1)
    c = _layer_norm(conv, cg_ref[...], cb_ref[...])
    c = c * _sigmoid(c)
    mix = _bdot(c, w2_ref[...]) + b2_ref[...]
    o_ref[...] = _layer_norm(ALPHA * x + mix, g_ref[...], be_ref[...])


def _conv_layer(x3, w1, b1, wd, bd, cg, cb, w2, b2, g, be):
    bsz, seq, d = x3.shape
    width = w2.shape[0]
    t = CONV_TILE
    blk = pl.BlockSpec((None, t, d), lambda b, i: (b, i, 0))
    consts = (w1, b1, wd, bd, cg, cb, w2, b2, g, be)
    return pl.pallas_call(
        functools.partial(_conv_kernel, width=width),
        grid=(bsz, seq // t),
        in_specs=[blk] + [_const_spec(c.shape) for c in consts],
        out_specs=blk,
        out_shape=jax.ShapeDtypeStruct((bsz, seq, d), F32),
        scratch_shapes=[pltpu.VMEM((width // LANES, CONV_HIST + t, LANES), F32),
                        pltpu.VMEM((width // LANES, t, LANES), F32)],
        compiler_params=_params("parallel", "arbitrary"),
        name="conformer_conv_ln",
    )(x3, *consts)


def _row(v):
    return v.astype(F32).reshape(1, -1)


def kernel(x, ln1_g, ln1_b, ln2_g, ln2_b, ffn_w1, ffn_b1, ffn_w2, ffn_b2, mix_w_in, mix_b_in, s5_lambda_re, s5_lambda_im, s5_log_dt, s5_b_re, s5_b_im, s5_c_re, s5_c_im, s5_d, s5_w_glu, s5_b_glu, mix_w_out, mix_b_out, conv_w_pw1, conv_b_pw1, conv_w_dw, conv_b_dw, conv_ln_g, conv_ln_b, conv_w_pw2, conv_b_pw2):
    bsz, seq, d = x.shape
    depth = ln1_g.shape[0]
    s5w = s5_d.shape[1]
    sbw = (mix_w_in.shape[2] - s5w) // 3
    n = bsz * seq
    x2 = x.reshape(n, d)
    for layer in range(depth):
        i = layer // 2
        ffn = (ffn_w1[layer].astype(BF16), _row(ffn_b1[layer]),
               ffn_w2[layer].astype(BF16), _row(ffn_b2[layer]),
               _row(ln2_g[layer]), _row(ln2_b[layer]))
        if layer % 2 == 0:
            u, qkv = _in_proj(x2, mix_w_in[i].astype(BF16), _row(mix_b_in[i]), s5w, sbw)
            sbo = _attention(qkv.reshape(bsz, seq, 3 * sbw), sbw)
            bm, cm, coef = _s5_tables(s5_lambda_re[i], s5_lambda_im[i], s5_log_dt[i],
                                      s5_b_re[i], s5_b_im[i], s5_c_re[i], s5_c_im[i])
            s5o = _s5(u.reshape(bsz, seq, s5w), bm, cm, coef, _row(s5_d[i]),
                      s5_w_glu[i].astype(BF16), _row(s5_b_glu[i]))
            x2 = _mix_ffn(x2, s5o.reshape(n, s5w), sbo.reshape(n, sbw),
                          mix_w_out[i].astype(BF16), _row(mix_b_out[i]),
                          _row(ln1_g[layer]), _row(ln1_b[layer]), *ffn)
        else:
            x1 = _conv_layer(x2.reshape(bsz, seq, d), conv_w_pw1[i].astype(BF16),
                             _row(conv_b_pw1[i]), conv_w_dw[i].astype(F32), _row(conv_b_dw[i]),
                             _row(conv_ln_g[i]), _row(conv_ln_b[i]),
                             conv_w_pw2[i].astype(BF16), _row(conv_b_pw2[i]),
                             _row(ln1_g[layer]), _row(ln1_b[layer]))
            x2 = _ffn(x1.reshape(n, d), *ffn)
    return x2.reshape(bsz, seq, d)
```

```python
import functools
import math

import numpy as np
import jax
import jax.numpy as jnp
from jax import lax
from jax.experimental import pallas as pl
from jax.experimental.pallas import tpu as pltpu

F32 = jnp.float32
BF16 = jnp.bfloat16

LANES = 128
SUBLANES = 8
VMEM_LIMIT = 56 * 1024 * 1024

S5_GROUP = 16
S5_STATE = 64
SB_HEAD_DIM = 64
CONV_SIZE = 31
LN_EPS = 1e-5
DEPTH = 2
ALPHA = (2 * DEPTH) ** 0.25

ROW_TILE = 512
FFN_ROWS = 1024
FFN_CHUNK = 1024
ATT_TILE = 256
ATT_HEADS = 4
ATT_STAGES = 4
ATT_SPLIT = 1
MASK_BIAS = -1e30
EXP2_CLAMP = 126.0
S5_TIME = 64
CONV_TILE = 256
CONV_HIST = 32


def _params(*sem):
    return pltpu.CompilerParams(dimension_semantics=sem, vmem_limit_bytes=VMEM_LIMIT)


def _const_spec(shape):
    nd = len(shape)
    return pl.BlockSpec(shape, lambda *_: (0,) * nd, pipeline_mode=pl.Buffered(1))


def _layer_norm(r, g, b):
    mu = jnp.mean(r, axis=-1, keepdims=True)
    c = r - mu
    var = jnp.mean(c * c, axis=-1, keepdims=True)
    return c * lax.rsqrt(var + LN_EPS) * g + b


def _bdot(a, w):
    return jnp.dot(a.astype(BF16), w, preferred_element_type=F32)


_sigmoid = jax.nn.sigmoid


def _in_proj_kernel(x_ref, w_ref, b_ref, u_ref, qkv_ref, *, s5w, sbw, scale):
    h = _bdot(x_ref[...], w_ref[...]) + b_ref[...]
    u_ref[...] = h[:, :s5w]
    qkv_ref[:, :sbw] = (h[:, s5w:s5w + sbw] * scale).astype(BF16)
    qkv_ref[:, sbw:] = h[:, s5w + sbw:].astype(BF16)


def _in_proj(x2, w, b, s5w, sbw):
    n, d = x2.shape
    wout = w.shape[1]
    kern = functools.partial(_in_proj_kernel, s5w=s5w, sbw=sbw,
                             scale=SB_HEAD_DIM ** -0.5 * math.log2(math.e))
    return pl.pallas_call(
        kern,
        grid=(n // ROW_TILE,),
        in_specs=[pl.BlockSpec((ROW_TILE, d), lambda i: (i, 0)),
                  _const_spec((d, wout)), _const_spec((1, wout))],
        out_specs=[pl.BlockSpec((ROW_TILE, s5w), lambda i: (i, 0)),
                   pl.BlockSpec((ROW_TILE, 3 * sbw), lambda i: (i, 0))],
        out_shape=[jax.ShapeDtypeStruct((n, s5w), F32),
                   jax.ShapeDtypeStruct((n, 3 * sbw), BF16)],
        compiler_params=_params("parallel"),
        name="in_proj",
    )(x2, w, b)


def _attn_tables(nblk, ncol):
    lead = ATT_STAGES - 1
    tiles = [(0, 0, 2, 0)] * lead
    for col in range(ncol):
        for r in range(nblk):
            tiles += [(r, c, 1 if c == r else 0, col) for c in range(r, -1, -1)]
    steps = -(-len(tiles) // ATT_STAGES) * ATT_STAGES
    tiles += [(0, 0, 2, 0)] * (steps + lead - len(tiles))
    return np.asarray(tiles, np.int32).T.copy(), steps


def _attn_kernel(tab_ref, q_ref, k_ref, v_ref, tri_ref, o_ref,
                 qm_ref, km_ref, vm_ref, z_ref, s_ref, w_ref, bias_ref, acc_ref, *, heads, steps):
    t = ATT_TILE
    width = heads * SB_HEAD_DIM
    seq = q_ref.shape[0]
    ncol = q_ref.shape[1] // width
    nblk = seq // t
    nt = (((1,), (1,)), ((), ()))
    lane = lax.broadcasted_iota(jnp.int32, (1, width), 1)
    masks = [(lane >= h * SB_HEAD_DIM) & (lane < (h + 1) * SB_HEAD_DIM) for h in range(heads)]
    zero = jnp.zeros((), BF16)
    for col in range(ncol):
        cols = slice(col * width, (col + 1) * width)
        for r in range(nblk):
            rows = slice(r * t, (r + 1) * t)
            qr = q_ref[rows, cols]
            vr = v_ref[rows, cols]
            km_ref[col * nblk + r] = k_ref[rows, cols]
            for h in range(heads):
                qm_ref[col * heads + h, rows, :] = jnp.where(masks[h], qr, zero)
                vm_ref[col * nblk + r, h * t:(h + 1) * t, :] = jnp.where(masks[h], vr, zero)
    @pl.when(pl.program_id(0) == 0)
    def _():
        row = lax.broadcasted_iota(jnp.int32, (t, t), 0)
        col = lax.broadcasted_iota(jnp.int32, (t, t), 1)
        bias_ref[0] = jnp.zeros((t, t), F32)
        bias_ref[1] = jnp.where(col < row, 0.0, MASK_BIAS)
        bias_ref[2] = jnp.full((t, t), MASK_BIAS, F32)
        z_ref[...] = jnp.full(z_ref.shape, MASK_BIAS, F32)
        s_ref[...] = jnp.zeros(s_ref.shape, BF16)
        w_ref[...] = jnp.zeros(w_ref.shape, BF16)

    acc_ref[...] = jnp.zeros(acc_ref.shape, F32)
    tri2 = tri_ref[...]

    def substep(n, r, carries):
        slot = r % ATT_STAGES
        rows = pl.ds(pl.multiple_of(tab_ref[0, n] * t, t), t)
        col = tab_ref[3, n]
        acc_ref[col, rows, :] += jnp.dot(w_ref[slot], vm_ref[col * nblk + tab_ref[1, n]],
                                         preferred_element_type=F32)
        slot = (r + 1) % ATT_STAGES
        row_start = tab_ref[2, n + 1] == 1
        new_carries = []
        for h in range(heads):
            c = jnp.where(row_start, 0.0, carries[h])
            incl = jnp.dot(s_ref[slot, h], tri2, preferred_element_type=F32)
            w_ref[slot, :, h * t:(h + 1) * t] = jnp.exp2((z_ref[slot, h] - c) - incl).astype(BF16)
            new_carries.append(c + incl[:, 0:1])
        slot = (r + 2) % ATT_STAGES
        for h in range(heads):
            z = z_ref[slot, h]
            sp = jnp.maximum(z, jnp.log2(1.0 + jnp.exp2(jnp.minimum(z, EXP2_CLAMP))))
            for p in range(ATT_SPLIT):
                part = sp.astype(BF16)
                s_ref[slot, h, :, p * t:(p + 1) * t] = part
                if p + 1 < ATT_SPLIT:
                    sp = sp - part.astype(F32)
        slot = (r + 3) % ATT_STAGES
        col = tab_ref[3, n + 3]
        kj = km_ref[col * nblk + tab_ref[1, n + 3]]
        bias = bias_ref[tab_ref[2, n + 3]]
        qrows = pl.ds(pl.multiple_of(tab_ref[0, n + 3] * t, t), t)
        for h in range(heads):
            z_ref[slot, h] = lax.dot_general(qm_ref[col * heads + h, qrows, :], kj, nt,
                                             preferred_element_type=F32) + bias
        return tuple(new_carries)

    def body(it, carries):
        for r in range(ATT_STAGES):
            carries = substep(ATT_STAGES * it + r, r, carries)
        return carries

    lax.fori_loop(0, steps // ATT_STAGES, body,
                  tuple(jnp.zeros((t, 1), F32) for _ in range(heads)))
    for col in range(ncol):
        o_ref[:, col * width:(col + 1) * width] = acc_ref[col].astype(BF16)


def _attention(qkv3, sbw):
    bsz, seq, _ = qkv3.shape
    width = ATT_HEADS * SB_HEAD_DIM
    ncol = sbw // width
    t = ATT_TILE
    nblk = seq // t
    table, steps = _attn_tables(nblk, ncol)
    tri = (lax.broadcasted_iota(jnp.int32, (ATT_SPLIT * t, t), 0) % t
           >= lax.broadcasted_iota(jnp.int32, (ATT_SPLIT * t, t), 1)).astype(BF16)
    grid_spec = pltpu.PrefetchScalarGridSpec(
        num_scalar_prefetch=1,
        grid=(bsz,),
        in_specs=[pl.BlockSpec((None, seq, sbw), lambda b, tab: (b, 0, 0)),
                  pl.BlockSpec((None, seq, sbw), lambda b, tab: (b, 0, 1)),
                  pl.BlockSpec((None, seq, sbw), lambda b, tab: (b, 0, 2)),
                  _const_spec((ATT_SPLIT * t, t))],
        out_specs=pl.BlockSpec((None, seq, sbw), lambda b, tab: (b, 0, 0)),
        scratch_shapes=[pltpu.VMEM((ncol * ATT_HEADS, seq, width), BF16),
                        pltpu.VMEM((ncol * nblk, t, width), BF16),
                        pltpu.VMEM((ncol * nblk, ATT_HEADS * t, width), BF16),
                        pltpu.VMEM((ATT_STAGES, ATT_HEADS, t, t), F32),
                        pltpu.VMEM((ATT_STAGES, ATT_HEADS, t, ATT_SPLIT * t), BF16),
                        pltpu.VMEM((ATT_STAGES, t, ATT_HEADS * t), BF16),
                        pltpu.VMEM((3, t, t), F32),
                        pltpu.VMEM((ncol, seq, width), F32)])
    return pl.pallas_call(
        functools.partial(_attn_kernel, heads=ATT_HEADS, steps=steps),
        grid_spec=grid_spec,
        out_shape=jax.ShapeDtypeStruct((bsz, seq, sbw), BF16),
        compiler_params=_params("arbitrary"),
        name="stickbreak_attn",
    )(jnp.asarray(table), qkv3, qkv3, qkv3, tri)


def _s5_tables(lam_re, lam_im, log_dt, b_re, b_im, c_re, c_im):
    g, n = lam_re.shape
    p = b_re.shape[-1]
    gh = g // 2
    npair = gh // 2
    lr = jnp.minimum(lam_re.astype(F32), -1e-4)
    li = lam_im.astype(F32)
    dt = jnp.exp(log_dt.astype(F32))[:, None]
    mag = jnp.exp(lr * dt)
    lb_re = mag * jnp.cos(li * dt)
    lb_im = mag * jnp.sin(li * dt)
    den = lr * lr + li * li
    nr, ni = lb_re - 1.0, lb_im
    cf_re = (nr * lr + ni * li) / den
    cf_im = (ni * lr - nr * li) / den
    bb_re = cf_re[..., None] * b_re - cf_im[..., None] * b_im
    bb_im = cf_re[..., None] * b_im + cf_im[..., None] * b_re
    eye_par = jnp.eye(2, dtype=F32)
    eye_pair = jnp.eye(npair, dtype=F32)

    tb = jnp.stack([bb_re, bb_im], 0).transpose(1, 3, 0, 2)
    tb = tb.reshape(2, npair, 2, p, 2, n)
    tb = tb[..., None, :] * eye_par[None, None, :, None, None, :, None]
    tb = tb.reshape(2, npair, 2 * p, 4 * n)
    bm = (tb[:, :, :, None, :] * eye_pair[None, :, None, :, None]).reshape(2, gh * p, gh * 2 * n)

    tc = jnp.stack([c_re, -c_im], 0).astype(F32)
    tc = tc.reshape(2, 2, npair, 2, p, n).transpose(1, 2, 0, 3, 5, 4)
    tc = tc[..., None, :] * eye_par[None, None, None, :, None, :, None]
    tc = tc.reshape(2, npair, 4 * n, 2 * p)
    cm = (tc[:, :, :, None, :] * eye_pair[None, :, None, :, None]).reshape(2, gh * 2 * n, gh * p)

    def cols(v):
        v = v.reshape(2, npair, 1, 2, n)
        return jnp.broadcast_to(v, (2, npair, 2, 2, n)).reshape(1, -1)

    coef = jnp.concatenate([jnp.broadcast_to(cols(lb_re), (SUBLANES, 2 * g * n)),
                            jnp.broadcast_to(cols(lb_im), (SUBLANES, 2 * g * n))], 0)
    return bm.astype(BF16), cm.astype(BF16), coef


def _gelu_tanh(x):
    return 0.5 * x * (1.0 + jnp.tanh(math.sqrt(2.0 / math.pi) * (x + 0.044715 * (x * x * x))))


def _s5_kernel(u_ref, bm_ref, cm_ref, coef_ref, d_ref, wg_ref, bg_ref, o_ref,
               utb_ref, bu_ref, st_ref, h_ref, otb_ref, *, nb, tt, width):
    nslab = width // LANES
    ncols = st_ref.shape[1]
    half = ncols // 2
    tiles = nb // SUBLANES

    @pl.when(pl.program_id(0) == 0)
    def _():
        h_ref[...] = jnp.zeros_like(h_ref)

    def seq_rows(b):
        return pl.ds((b // SUBLANES) * tt * SUBLANES + b % SUBLANES, tt, stride=SUBLANES)

    for b in range(nb):
        for s in range(nslab):
            utb_ref[s, seq_rows(b), :] = u_ref[b, :, s * LANES:(s + 1) * LANES]
    u_tb = jnp.concatenate([utb_ref[s] for s in range(nslab)], axis=1)

    hw = width // 2
    for hf in range(2):
        bu_ref[:, hf * half:(hf + 1) * half] = _bdot(u_tb[:, hf * hw:(hf + 1) * hw], bm_ref[hf])

    pairs = 4
    for cg in range(ncols // (2 * LANES * pairs)):
        base = cg * 2 * LANES * pairs
        re_c = [base + q * 2 * LANES for q in range(pairs)]
        im_c = [c + LANES for c in re_c]
        a_re = [coef_ref[0:SUBLANES, c:c + LANES] for c in re_c]
        a_im = [coef_ref[SUBLANES:2 * SUBLANES, c:c + LANES] for c in re_c]
        h0 = tuple(h_ref[s * SUBLANES:(s + 1) * SUBLANES, c:c + LANES]
                   for q in range(pairs) for c in (re_c[q], im_c[q]) for s in range(tiles))

        def step(t, h, re_c=re_c, im_c=im_c, a_re=a_re, a_im=a_im):
            res = [None] * (2 * pairs * tiles)
            r0 = pl.multiple_of(t * SUBLANES, SUBLANES)
            for q in range(pairs):
                for s in range(tiles):
                    rs = pl.ds(r0 + s * tt * SUBLANES, SUBLANES)
                    hr = h[(2 * q) * tiles + s]
                    hi = h[(2 * q + 1) * tiles + s]
                    nr = a_re[q] * hr - a_im[q] * hi + bu_ref[rs, re_c[q]:re_c[q] + LANES]
                    ni = a_re[q] * hi + a_im[q] * hr + bu_ref[rs, im_c[q]:im_c[q] + LANES]
                    st_ref[rs, re_c[q]:re_c[q] + LANES] = nr
                    st_ref[rs, im_c[q]:im_c[q] + LANES] = ni
                    res[(2 * q) * tiles + s] = nr
                    res[(2 * q + 1) * tiles + s] = ni
            return tuple(res)

        hl = lax.fori_loop(0, tt, step, h0, unroll=4)
        k = 0
        for q in range(pairs):
            for c in (re_c[q], im_c[q]):
                for s in range(tiles):
                    h_ref[s * SUBLANES:(s + 1) * SUBLANES, c:c + LANES] = hl[k]
                    k += 1

    y = jnp.concatenate(
        [_bdot(st_ref[:, hf * half:(hf + 1) * half], cm_ref[hf]) for hf in range(2)], axis=1)
    y = _gelu_tanh(y + d_ref[...] * u_tb)
    hg = _bdot(y, wg_ref[...]) + bg_ref[...]
    out = hg[:, :width] * _sigmoid(hg[:, width:])
    for s in range(nslab):
        otb_ref[s] = out[:, s * LANES:(s + 1) * LANES]
    for b in range(nb):
        o_ref[b] = jnp.concatenate(
            [otb_ref[s, seq_rows(b), :] for s in range(nslab)], axis=1).astype(BF16)


def _s5(u3, bm, cm, coef, d, wg, bg):
    nb, seq, width = u3.shape
    tt = S5_TIME
    ncols = coef.shape[1]
    kern = functools.partial(_s5_kernel, nb=nb, tt=tt, width=width)
    return pl.pallas_call(
        kern,
        grid=(seq // tt,),
        in_specs=[pl.BlockSpec((nb, tt, width), lambda i: (0, i, 0)),
                  _const_spec(bm.shape), _const_spec(cm.shape), _const_spec(coef.shape),
                  _const_spec(d.shape), _const_spec(wg.shape), _const_spec(bg.shape)],
        out_specs=pl.BlockSpec((nb, tt, width), lambda i: (0, i, 0)),
        out_shape=jax.ShapeDtypeStruct((nb, seq, width), BF16),
        scratch_shapes=[pltpu.VMEM((width // LANES, tt * nb, LANES), F32),
                        pltpu.VMEM((tt * nb, ncols), F32),
                        pltpu.VMEM((tt * nb, ncols), F32),
                        pltpu.VMEM((nb, ncols), F32),
                        pltpu.VMEM((width // LANES, tt * nb, LANES), F32)],
        compiler_params=_params("arbitrary"),
        name="s5_mixer",
    )(u3, bm, cm, coef, d, wg, bg)


def _ffn_tail(x, w1_ref, b1_ref, w2_ref, b2_ref, g_ref, be_ref):
    xb = x.astype(BF16)
    acc = ALPHA * x + b2_ref[...]
    for c in range(w1_ref.shape[1] // FFN_CHUNK):
        cs = slice(c * FFN_CHUNK, (c + 1) * FFN_CHUNK)
        h = jnp.dot(xb, w1_ref[:, cs], preferred_element_type=F32) + b1_ref[:, cs]
        h = jnp.square(jnp.maximum(h, 0.0))
        acc = acc + jnp.dot(h.astype(BF16), w2_ref[cs, :], preferred_element_type=F32)
    return _layer_norm(acc, g_ref[...], be_ref[...])


def _ffn_kernel(x_ref, w1_ref, b1_ref, w2_ref, b2_ref, g_ref, be_ref, o_ref):
    o_ref[...] = _ffn_tail(x_ref[...], w1_ref, b1_ref, w2_ref, b2_ref, g_ref, be_ref)


def _mix_ffn_kernel(x_ref, a_ref, s_ref, wo_ref, bo_ref, g1_ref, be1_ref,
                    w1_ref, b1_ref, w2_ref, b2_ref, g2_ref, be2_ref, o_ref, *, s5w):
    mix = (jnp.dot(a_ref[...], wo_ref[:s5w, :], preferred_element_type=F32)
           + jnp.dot(s_ref[...], wo_ref[s5w:, :], preferred_element_type=F32) + bo_ref[...])
    x1 = _layer_norm(ALPHA * x_ref[...] + mix, g1_ref[...], be1_ref[...])
    o_ref[...] = _ffn_tail(x1, w1_ref, b1_ref, w2_ref, b2_ref, g2_ref, be2_ref)


def _row_spec(width):
    return pl.BlockSpec((FFN_ROWS, width), lambda i: (i, 0))


def _ffn(x2, w1, b1, w2, b2, g, be):
    n, d = x2.shape
    consts = (w1, b1, w2, b2, g, be)
    return pl.pallas_call(
        _ffn_kernel,
        grid=(n // FFN_ROWS,),
        in_specs=[_row_spec(d)] + [_const_spec(c.shape) for c in consts],
        out_specs=_row_spec(d),
        out_shape=jax.ShapeDtypeStruct((n, d), F32),
        compiler_params=_params("parallel"),
        name="ffn_ln",
    )(x2, *consts)


def _mix_ffn(x2, s5o, sbo, wo, bo, g1, be1, w1, b1, w2, b2, g2, be2):
    n, d = x2.shape
    s5w, sbw = s5o.shape[1], sbo.shape[1]
    consts = (wo, bo, g1, be1, w1, b1, w2, b2, g2, be2)
    return pl.pallas_call(
        functools.partial(_mix_ffn_kernel, s5w=s5w),
        grid=(n // FFN_ROWS,),
        in_specs=[_row_spec(d), _row_spec(s5w), _row_spec(sbw)]
                 + [_const_spec(c.shape) for c in consts],
        out_specs=_row_spec(d),
        out_shape=jax.ShapeDtypeStruct((n, d), F32),
        compiler_params=_params("parallel"),
        name="out_proj_ffn_ln",
    )(x2, s5o, sbo, *consts)


def _conv_kernel(x_ref, w1_ref, b1_ref, wd_ref, bd_ref, cg_ref, cb_ref, w2_ref, b2_ref,
                 g_ref, be_ref, o_ref, buf_ref, cv_ref, *, width):
    t = CONV_TILE
    nslab = width // LANES

    @pl.when(pl.program_id(1) == 0)
    def _():
        buf_ref[:, 0:CONV_HIST, :] = jnp.zeros((nslab, CONV_HIST, LANES), F32)

    x = x_ref[...]
    h = _bdot(x, w1_ref[...]) + b1_ref[...]
    glu = h[:, :width] * _sigmoid(h[:, width:])
    for cb in range(nslab):
        buf_ref[cb, CONV_HIST:CONV_HIST + t, :] = glu[:, cb * LANES:(cb + 1) * LANES]

    first = CONV_HIST - (CONV_SIZE - 1)
    for cb in range(nslab):
        cs = slice(cb * LANES, (cb + 1) * LANES)
        for par in range(2):
            acc = jnp.broadcast_to(bd_ref[:, cs], (t // 2, LANES))
            for k in range(CONV_SIZE):
                acc = acc + wd_ref[k:k + 1, cs] * buf_ref[cb, pl.ds(first + k + par, t // 2, stride=2), :]
            cv_ref[cb, pl.ds(par, t // 2, stride=2), :] = acc
        buf_ref[cb, 0:CONV_HIST, :] = buf_ref[cb, t:t + CONV_HIST, :]

    conv = jnp.concatenate([cv_ref[cb] for cb in range(nslab)], axis=1)
    c = _layer_norm(conv, cg_ref[...], cb_ref[...])
    c = c * _sigmoid(c)
    mix = _bdot(c, w2_ref[...]) + b2_ref[...]
    o_ref[...] = _layer_norm(ALPHA * x + mix, g_ref[...], be_ref[...])


def _conv_layer(x3, w1, b1, wd, bd, cg, cb, w2, b2, g, be):
    bsz, seq, d = x3.shape
    width = w2.shape[0]
    t = CONV_TILE
    blk = pl.BlockSpec((None, t, d), lambda b, i: (b, i, 0))
    consts = (w1, b1, wd, bd, cg, cb, w2, b2, g, be)
    return pl.pallas_call(
        functools.partial(_conv_kernel, width=width),
        grid=(bsz, seq // t),
        in_specs=[blk] + [_const_spec(c.shape) for c in consts],
        out_specs=blk,
        out_shape=jax.ShapeDtypeStruct((bsz, seq, d), F32),
        scratch_shapes=[pltpu.VMEM((width // LANES, CONV_HIST + t, LANES), F32),
                        pltpu.VMEM((width // LANES, t, LANES), F32)],
        compiler_params=_params("parallel", "arbitrary"),
        name="conformer_conv_ln",
    )(x3, *consts)


def _row(v):
    return v.astype(F32).reshape(1, -1)


def kernel(x, ln1_g, ln1_b, ln2_g, ln2_b, ffn_w1, ffn_b1, ffn_w2, ffn_b2, mix_w_in, mix_b_in, s5_lambda_re, s5_lambda_im, s5_log_dt, s5_b_re, s5_b_im, s5_c_re, s5_c_im, s5_d, s5_w_glu, s5_b_glu, mix_w_out, mix_b_out, conv_w_pw1, conv_b_pw1, conv_w_dw, conv_b_dw, conv_ln_g, conv_ln_b, conv_w_pw2, conv_b_pw2):
    bsz, seq, d = x.shape
    depth = ln1_g.shape[0]
    s5w = s5_d.shape[1]
    sbw = (mix_w_in.shape[2] - s5w) // 3
    n = bsz * seq
    x2 = x.reshape(n, d)
    for layer in range(depth):
        i = layer // 2
        ffn = (ffn_w1[layer].astype(BF16), _row(ffn_b1[layer]),
               ffn_w2[layer].astype(BF16), _row(ffn_b2[layer]),
               _row(ln2_g[layer]), _row(ln2_b[layer]))
        if layer % 2 == 0:
            u, qkv = _in_proj(x2, mix_w_in[i].astype(BF16), _row(mix_b_in[i]), s5w, sbw)
            sbo = _attention(qkv.reshape(bsz, seq, 3 * sbw), sbw)
            bm, cm, coef = _s5_tables(s5_lambda_re[i], s5_lambda_im[i], s5_log_dt[i],
                                      s5_b_re[i], s5_b_im[i], s5_c_re[i], s5_c_im[i])
            s5o = _s5(u.reshape(bsz, seq, s5w), bm, cm, coef, _row(s5_d[i]),
                      s5_w_glu[i].astype(BF16), _row(s5_b_glu[i]))
            x2 = _mix_ffn(x2, s5o.reshape(n, s5w), sbo.reshape(n, sbw),
                          mix_w_out[i].astype(BF16), _row(mix_b_out[i]),
                          _row(ln1_g[layer]), _row(ln1_b[layer]), *ffn)
        else:
            x1 = _conv_layer(x2.reshape(bsz, seq, d), conv_w_pw1[i].astype(BF16),
                             _row(conv_b_pw1[i]), conv_w_dw[i].astype(F32), _row(conv_b_dw[i]),
                             _row(conv_ln_g[i]), _row(conv_ln_b[i]),
                             conv_w_pw2[i].astype(BF16), _row(conv_b_pw2[i]),
                             _row(ln1_g[layer]), _row(ln1_b[layer]))
            x2 = _ffn(x1.reshape(n, d), *ffn)
    return x2.reshape(bsz, seq, d)
```

```python
import functools
import math

import numpy as np
import jax
import jax.numpy as jnp
from jax import lax
from jax.experimental import pallas as pl
from jax.experimental.pallas import tpu as pltpu

F32 = jnp.float32
BF16 = jnp.bfloat16

LANES = 128
SUBLANES = 8
VMEM_LIMIT = 56 * 1024 * 1024

S5_GROUP = 16
S5_STATE = 64
SB_HEAD_DIM = 64
CONV_SIZE = 31
LN_EPS = 1e-5
DEPTH = 2
ALPHA = (2 * DEPTH) ** 0.25

ROW_TILE = 1024
FFN_ROWS = 1024
FFN_CHUNK = 1024
ATT_TILE = 256
ATT_HEADS = 4
ATT_STAGES = 4
ATT_SPLIT = 1
MASK_BIAS = -1e30
EXP2_CLAMP = 126.0
S5_TIME = 64
CONV_TILE = 1024
CONV_HIST = 32


def _params(*sem):
    return pltpu.CompilerParams(dimension_semantics=sem, vmem_limit_bytes=VMEM_LIMIT)


def _const_spec(shape):
    nd = len(shape)
    return pl.BlockSpec(shape, lambda *_: (0,) * nd, pipeline_mode=pl.Buffered(1))


def _layer_norm(r, g, b):
    mu = jnp.mean(r, axis=-1, keepdims=True)
    c = r - mu
    var = jnp.mean(c * c, axis=-1, keepdims=True)
    return c * lax.rsqrt(var + LN_EPS) * g + b


def _bdot(a, w):
    return jnp.dot(a.astype(BF16), w, preferred_element_type=F32)


_sigmoid = jax.nn.sigmoid


def _in_proj_kernel(x_ref, w_ref, b_ref, u_ref, qkv_ref, *, s5w, sbw, scale):
    h = _bdot(x_ref[...], w_ref[...]) + b_ref[...]
    u_ref[...] = h[:, :s5w]
    qkv_ref[:, :sbw] = (h[:, s5w:s5w + sbw] * scale).astype(BF16)
    qkv_ref[:, sbw:] = h[:, s5w + sbw:].astype(BF16)


def _in_proj(x2, w, b, s5w, sbw):
    n, d = x2.shape
    wout = w.shape[1]
    kern = functools.partial(_in_proj_kernel, s5w=s5w, sbw=sbw,
                             scale=SB_HEAD_DIM ** -0.5 * math.log2(math.e))
    return pl.pallas_call(
        kern,
        grid=(n // ROW_TILE,),
        in_specs=[pl.BlockSpec((ROW_TILE, d), lambda i: (i, 0)),
                  _const_spec((d, wout)), _const_spec((1, wout))],
        out_specs=[pl.BlockSpec((ROW_TILE, s5w), lambda i: (i, 0)),
                   pl.BlockSpec((ROW_TILE, 3 * sbw), lambda i: (i, 0))],
        out_shape=[jax.ShapeDtypeStruct((n, s5w), F32),
                   jax.ShapeDtypeStruct((n, 3 * sbw), BF16)],
        compiler_params=_params("parallel"),
        name="in_proj",
    )(x2, w, b)


def _attn_tables(nblk, ncol):
    lead = ATT_STAGES - 1
    tiles = [(0, 0, 2, 0)] * lead
    for col in range(ncol):
        for r in range(nblk):
            tiles += [(r, c, 1 if c == r else 0, col) for c in range(r, -1, -1)]
    steps = -(-len(tiles) // ATT_STAGES) * ATT_STAGES
    tiles += [(0, 0, 2, 0)] * (steps + lead - len(tiles))
    return np.asarray(tiles, np.int32).T.copy(), steps


def _attn_kernel(tab_ref, q_ref, k_ref, v_ref, tri_ref, o_ref,
                 qm_ref, km_ref, vm_ref, z_ref, s_ref, w_ref, bias_ref, acc_ref, *, heads, steps):
    t = ATT_TILE
    width = heads * SB_HEAD_DIM
    seq = q_ref.shape[0]
    ncol = q_ref.shape[1] // width
    nblk = seq // t
    nt = (((1,), (1,)), ((), ()))
    lane = lax.broadcasted_iota(jnp.int32, (1, width), 1)
    masks = [(lane >= h * SB_HEAD_DIM) & (lane < (h + 1) * SB_HEAD_DIM) for h in range(heads)]
    zero = jnp.zeros((), BF16)
    for col in range(ncol):
        cols = slice(col * width, (col + 1) * width)
        for r in range(nblk):
            rows = slice(r * t, (r + 1) * t)
            qr = q_ref[rows, cols]
            vr = v_ref[rows, cols]
            km_ref[col * nblk + r] = k_ref[rows, cols]
            for h in range(heads):
                qm_ref[col * heads + h, rows, :] = jnp.where(masks[h], qr, zero)
                vm_ref[col * nblk + r, h * t:(h + 1) * t, :] = jnp.where(masks[h], vr, zero)
    @pl.when(pl.program_id(0) == 0)
    def _():
        row = lax.broadcasted_iota(jnp.int32, (t, t), 0)
        col = lax.broadcasted_iota(jnp.int32, (t, t), 1)
        bias_ref[0] = jnp.zeros((t, t), F32)
        bias_ref[1] = jnp.where(col < row, 0.0, MASK_BIAS)
        bias_ref[2] = jnp.full((t, t), MASK_BIAS, F32)
        z_ref[...] = jnp.full(z_ref.shape, MASK_BIAS, F32)
        s_ref[...] = jnp.zeros(s_ref.shape, BF16)
        w_ref[...] = jnp.zeros(w_ref.shape, BF16)

    acc_ref[...] = jnp.zeros(acc_ref.shape, F32)
    tri2 = tri_ref[...]

    def substep(n, r, carries):
        slot = r % ATT_STAGES
        rows = pl.ds(pl.multiple_of(tab_ref[0, n] * t, t), t)
        col = tab_ref[3, n]
        acc_ref[col, rows, :] += jnp.dot(w_ref[slot], vm_ref[col * nblk + tab_ref[1, n]],
                                         preferred_element_type=F32)
        slot = (r + 1) % ATT_STAGES
        row_start = tab_ref[2, n + 1] == 1
        new_carries = []
        for h in range(heads):
            c = jnp.where(row_start, 0.0, carries[h])
            incl = jnp.dot(s_ref[slot, h], tri2, preferred_element_type=F32)
            w_ref[slot, :, h * t:(h + 1) * t] = jnp.exp2((z_ref[slot, h] - c) - incl).astype(BF16)
            new_carries.append(c + incl[:, 0:1])
        slot = (r + 2) % ATT_STAGES
        for h in range(heads):
            z = z_ref[slot, h]
            sp = jnp.maximum(z, jnp.log2(1.0 + jnp.exp2(jnp.minimum(z, EXP2_CLAMP))))
            for p in range(ATT_SPLIT):
                part = sp.astype(BF16)
                s_ref[slot, h, :, p * t:(p + 1) * t] = part
                if p + 1 < ATT_SPLIT:
                    sp = sp - part.astype(F32)
        slot = (r + 3) % ATT_STAGES
        col = tab_ref[3, n + 3]
        kj = km_ref[col * nblk + tab_ref[1, n + 3]]
        bias = bias_ref[tab_ref[2, n + 3]]
        qrows = pl.ds(pl.multiple_of(tab_ref[0, n + 3] * t, t), t)
        for h in range(heads):
            z_ref[slot, h] = lax.dot_general(qm_ref[col * heads + h, qrows, :], kj, nt,
                                             preferred_element_type=F32) + bias
        return tuple(new_carries)

    def body(it, carries):
        for r in range(ATT_STAGES):
            carries = substep(ATT_STAGES * it + r, r, carries)
        return carries

    lax.fori_loop(0, steps // ATT_STAGES, body,
                  tuple(jnp.zeros((t, 1), F32) for _ in range(heads)))
    for col in range(ncol):
        o_ref[:, col * width:(col + 1) * width] = acc_ref[col].astype(BF16)


def _attention(qkv3, sbw):
    bsz, seq, _ = qkv3.shape
    width = ATT_HEADS * SB_HEAD_DIM
    ncol = sbw // width
    t = ATT_TILE
    nblk = seq // t
    table, steps = _attn_tables(nblk, ncol)
    tri = (lax.broadcasted_iota(jnp.int32, (ATT_SPLIT * t, t), 0) % t
           >= lax.broadcasted_iota(jnp.int32, (ATT_SPLIT * t, t), 1)).astype(BF16)
    grid_spec = pltpu.PrefetchScalarGridSpec(
        num_scalar_prefetch=1,
        grid=(bsz,),
        in_specs=[pl.BlockSpec((None, seq, sbw), lambda b, tab: (b, 0, 0)),
                  pl.BlockSpec((None, seq, sbw), lambda b, tab: (b, 0, 1)),
                  pl.BlockSpec((None, seq, sbw), lambda b, tab: (b, 0, 2)),
                  _const_spec((ATT_SPLIT * t, t))],
        out_specs=pl.BlockSpec((None, seq, sbw), lambda b, tab: (b, 0, 0)),
        scratch_shapes=[pltpu.VMEM((ncol * ATT_HEADS, seq, width), BF16),
                        pltpu.VMEM((ncol * nblk, t, width), BF16),
                        pltpu.VMEM((ncol * nblk, ATT_HEADS * t, width), BF16),
                        pltpu.VMEM((ATT_STAGES, ATT_HEADS, t, t), F32),
                        pltpu.VMEM((ATT_STAGES, ATT_HEADS, t, ATT_SPLIT * t), BF16),
                        pltpu.VMEM((ATT_STAGES, t, ATT_HEADS * t), BF16),
                        pltpu.VMEM((3, t, t), F32),
                        pltpu.VMEM((ncol, seq, width), F32)])
    return pl.pallas_call(
        functools.partial(_attn_kernel, heads=ATT_HEADS, steps=steps),
        grid_spec=grid_spec,
        out_shape=jax.ShapeDtypeStruct((bsz, seq, sbw), BF16),
        compiler_params=_params("arbitrary"),
        name="stickbreak_attn",
    )(jnp.asarray(table), qkv3, qkv3, qkv3, tri)


def _s5_tables(lam_re, lam_im, log_dt, b_re, b_im, c_re, c_im):
    g, n = lam_re.shape
    p = b_re.shape[-1]
    gh = g // 2
    npair = gh // 2
    lr = jnp.minimum(lam_re.astype(F32), -1e-4)
    li = lam_im.astype(F32)
    dt = jnp.exp(log_dt.astype(F32))[:, None]
    mag = jnp.exp(lr * dt)
    lb_re = mag * jnp.cos(li * dt)
    lb_im = mag * jnp.sin(li * dt)
    den = lr * lr + li * li
    nr, ni = lb_re - 1.0, lb_im
    cf_re = (nr * lr + ni * li) / den
    cf_im = (ni * lr - nr * li) / den
    bb_re = cf_re[..., None] * b_re - cf_im[..., None] * b_im
    bb_im = cf_re[..., None] * b_im + cf_im[..., None] * b_re
    eye_par = jnp.eye(2, dtype=F32)
    eye_pair = jnp.eye(npair, dtype=F32)

    tb = jnp.stack([bb_re, bb_im], 0).transpose(1, 3, 0, 2)
    tb = tb.reshape(2, npair, 2, p, 2, n)
    tb = tb[..., None, :] * eye_par[None, None, :, None, None, :, None]
    tb = tb.reshape(2, npair, 2 * p, 4 * n)
    bm = (tb[:, :, :, None, :] * eye_pair[None, :, None, :, None]).reshape(2, gh * p, gh * 2 * n)

    tc = jnp.stack([c_re, -c_im], 0).astype(F32)
    tc = tc.reshape(2, 2, npair, 2, p, n).transpose(1, 2, 0, 3, 5, 4)
    tc = tc[..., None, :] * eye_par[None, None, None, :, None, :, None]
    tc = tc.reshape(2, npair, 4 * n, 2 * p)
    cm = (tc[:, :, :, None, :] * eye_pair[None, :, None, :, None]).reshape(2, gh * 2 * n, gh * p)

    def cols(v):
        v = v.reshape(2, npair, 1, 2, n)
        return jnp.broadcast_to(v, (2, npair, 2, 2, n)).reshape(1, -1)

    coef = jnp.concatenate([jnp.broadcast_to(cols(lb_re), (SUBLANES, 2 * g * n)),
                            jnp.broadcast_to(cols(lb_im), (SUBLANES, 2 * g * n))], 0)
    return bm.astype(BF16), cm.astype(BF16), coef


def _gelu_tanh(x):
    return 0.5 * x * (1.0 + jnp.tanh(math.sqrt(2.0 / math.pi) * (x + 0.044715 * (x * x * x))))


def _s5_kernel(u_ref, bm_ref, cm_ref, coef_ref, d_ref, wg_ref, bg_ref, o_ref,
               utb_ref, bu_ref, st_ref, h_ref, otb_ref, *, nb, tt, width):
    nslab = width // LANES
    ncols = st_ref.shape[1]
    half = ncols // 2
    tiles = nb // SUBLANES

    @pl.when(pl.program_id(0) == 0)
    def _():
        h_ref[...] = jnp.zeros_like(h_ref)

    def seq_rows(b):
        return pl.ds((b // SUBLANES) * tt * SUBLANES + b % SUBLANES, tt, stride=SUBLANES)

    for b in range(nb):
        for s in range(nslab):
            utb_ref[s, seq_rows(b), :] = u_ref[b, :, s * LANES:(s + 1) * LANES]
    u_tb = jnp.concatenate([utb_ref[s] for s in range(nslab)], axis=1)

    hw = width // 2
    for hf in range(2):
        bu_ref[:, hf * half:(hf + 1) * half] = _bdot(u_tb[:, hf * hw:(hf + 1) * hw], bm_ref[hf])

    pairs = 4
    for cg in range(ncols // (2 * LANES * pairs)):
        base = cg * 2 * LANES * pairs
        re_c = [base + q * 2 * LANES for q in range(pairs)]
        im_c = [c + LANES for c in re_c]
        a_re = [coef_ref[0:SUBLANES, c:c + LANES] for c in re_c]
        a_im = [coef_ref[SUBLANES:2 * SUBLANES, c:c + LANES] for c in re_c]
        h0 = tuple(h_ref[s * SUBLANES:(s + 1) * SUBLANES, c:c + LANES]
                   for q in range(pairs) for c in (re_c[q], im_c[q]) for s in range(tiles))

        def step(t, h, re_c=re_c, im_c=im_c, a_re=a_re, a_im=a_im):
            res = [None] * (2 * pairs * tiles)
            r0 = pl.multiple_of(t * SUBLANES, SUBLANES)
            for q in range(pairs):
                for s in range(tiles):
                    rs = pl.ds(r0 + s * tt * SUBLANES, SUBLANES)
                    hr = h[(2 * q) * tiles + s]
                    hi = h[(2 * q + 1) * tiles + s]
                    nr = a_re[q] * hr - a_im[q] * hi + bu_ref[rs, re_c[q]:re_c[q] + LANES]
                    ni = a_re[q] * hi + a_im[q] * hr + bu_ref[rs, im_c[q]:im_c[q] + LANES]
                    st_ref[rs, re_c[q]:re_c[q] + LANES] = nr
                    st_ref[rs, im_c[q]:im_c[q] + LANES] = ni
                    res[(2 * q) * tiles + s] = nr
                    res[(2 * q + 1) * tiles + s] = ni
            return tuple(res)

        hl = lax.fori_loop(0, tt, step, h0, unroll=4)
        k = 0
        for q in range(pairs):
            for c in (re_c[q], im_c[q]):
                for s in range(tiles):
                    h_ref[s * SUBLANES:(s + 1) * SUBLANES, c:c + LANES] = hl[k]
                    k += 1

    y = jnp.concatenate(
        [_bdot(st_ref[:, hf * half:(hf + 1) * half], cm_ref[hf]) for hf in range(2)], axis=1)
    y = _gelu_tanh(y + d_ref[...] * u_tb)
    hg = _bdot(y, wg_ref[...]) + bg_ref[...]
    out = hg[:, :width] * _sigmoid(hg[:, width:])
    for s in range(nslab):
        otb_ref[s] = out[:, s * LANES:(s + 1) * LANES]
    for b in range(nb):
        o_ref[b] = jnp.concatenate(
            [otb_ref[s, seq_rows(b), :] for s in range(nslab)], axis=1).astype(BF16)


def _s5(u3, bm, cm, coef, d, wg, bg):
    nb, seq, width = u3.shape
    tt = S5_TIME
    ncols = coef.shape[1]
    kern = functools.partial(_s5_kernel, nb=nb, tt=tt, width=width)
    return pl.pallas_call(
        kern,
        grid=(seq // tt,),
        in_specs=[pl.BlockSpec((nb, tt, width), lambda i: (0, i, 0)),
                  _const_spec(bm.shape), _const_spec(cm.shape), _const_spec(coef.shape),
                  _const_spec(d.shape), _const_spec(wg.shape), _const_spec(bg.shape)],
        out_specs=pl.BlockSpec((nb, tt, width), lambda i: (0, i, 0)),
        out_shape=jax.ShapeDtypeStruct((nb, seq, width), BF16),
        scratch_shapes=[pltpu.VMEM((width // LANES, tt * nb, LANES), F32),
                        pltpu.VMEM((tt * nb, ncols), F32),
                        pltpu.VMEM((tt * nb, ncols), F32),
                        pltpu.VMEM((nb, ncols), F32),
                        pltpu.VMEM((width // LANES, tt * nb, LANES), F32)],
        compiler_params=_params("arbitrary"),
        name="s5_mixer",
    )(u3, bm, cm, coef, d, wg, bg)


def _ffn_tail(x, w1_ref, b1_ref, w2_ref, b2_ref, g_ref, be_ref):
    xb = x.astype(BF16)
    acc = ALPHA * x + b2_ref[...]
    for c in range(w1_ref.shape[1] // FFN_CHUNK):
        cs = slice(c * FFN_CHUNK, (c + 1) * FFN_CHUNK)
        h = jnp.dot(xb, w1_ref[:, cs], preferred_element_type=F32) + b1_ref[:, cs]
        h = jnp.square(jnp.maximum(h, 0.0))
        acc = acc + jnp.dot(h.astype(BF16), w2_ref[cs, :], preferred_element_type=F32)
    return _layer_norm(acc, g_ref[...], be_ref[...])


def _ffn_kernel(x_ref, w1_ref, b1_ref, w2_ref, b2_ref, g_ref, be_ref, o_ref):
    o_ref[...] = _ffn_tail(x_ref[...], w1_ref, b1_ref, w2_ref, b2_ref, g_ref, be_ref)


def _mix_ffn_kernel(x_ref, a_ref, s_ref, wo_ref, bo_ref, g1_ref, be1_ref,
                    w1_ref, b1_ref, w2_ref, b2_ref, g2_ref, be2_ref, o_ref, *, s5w):
    mix = (jnp.dot(a_ref[...], wo_ref[:s5w, :], preferred_element_type=F32)
           + jnp.dot(s_ref[...], wo_ref[s5w:, :], preferred_element_type=F32) + bo_ref[...])
    x1 = _layer_norm(ALPHA * x_ref[...] + mix, g1_ref[...], be1_ref[...])
    o_ref[...] = _ffn_tail(x1, w1_ref, b1_ref, w2_ref, b2_ref, g2_ref, be2_ref)


def _row_spec(width):
    return pl.BlockSpec((FFN_ROWS, width), lambda i: (i, 0))


def _ffn(x2, w1, b1, w2, b2, g, be):
    n, d = x2.shape
    consts = (w1, b1, w2, b2, g, be)
    return pl.pallas_call(
        _ffn_kernel,
        grid=(n // FFN_ROWS,),
        in_specs=[_row_spec(d)] + [_const_spec(c.shape) for c in consts],
        out_specs=_row_spec(d),
        out_shape=jax.ShapeDtypeStruct((n, d), F32),
        compiler_params=_params("parallel"),
        name="ffn_ln",
    )(x2, *consts)


def _mix_ffn(x2, s5o, sbo, wo, bo, g1, be1, w1, b1, w2, b2, g2, be2):
    n, d = x2.shape
    s5w, sbw = s5o.shape[1], sbo.shape[1]
    consts = (wo, bo, g1, be1, w1, b1, w2, b2, g2, be2)
    return pl.pallas_call(
        functools.partial(_mix_ffn_kernel, s5w=s5w),
        grid=(n // FFN_ROWS,),
        in_specs=[_row_spec(d), _row_spec(s5w), _row_spec(sbw)]
                 + [_const_spec(c.shape) for c in consts],
        out_specs=_row_spec(d),
        out_shape=jax.ShapeDtypeStruct((n, d), F32),
        compiler_params=_params("parallel"),
        name="out_proj_ffn_ln",
    )(x2, s5o, sbo, *consts)


def _conv_kernel(x_ref, w1_ref, b1_ref, wd_ref, bd_ref, cg_ref, cb_ref, w2_ref, b2_ref,
                 g_ref, be_ref, o_ref, buf_ref, cv_ref, *, width):
    t = CONV_TILE
    nslab = width // LANES

    @pl.when(pl.program_id(1) == 0)
    def _():
        buf_ref[:, 0:CONV_HIST, :] = jnp.zeros((nslab, CONV_HIST, LANES), F32)

    x = x_ref[...]
    h = _bdot(x, w1_ref[...]) + b1_ref[...]
    glu = h[:, :width] * _sigmoid(h[:, width:])
    for cb in range(nslab):
        buf_ref[cb, CONV_HIST:CONV_HIST + t, :] = glu[:, cb * LANES:(cb + 1) * LANES]

    first = CONV_HIST - (CONV_SIZE - 1)
    for cb in range(nslab):
        cs = slice(cb * LANES, (cb + 1) * LANES)
        for par in range(2):
            acc = jnp.broadcast_to(bd_ref[:, cs], (t // 2, LANES))
            for k in range(CONV_SIZE):
                acc = acc + wd_ref[k:k + 1, cs] * buf_ref[cb, pl.ds(first + k + par, t // 2, stride=2), :]
            cv_ref[cb, pl.ds(par, t // 2, stride=2), :] = acc
        buf_ref[cb, 0:CONV_HIST, :] = buf_ref[cb, t:t + CONV_HIST, :]

    conv = jnp.concatenate([cv_ref[cb] for cb in range(nslab)], axis=1)
    c = _layer_norm(conv, cg_ref[...], cb_ref[...])
    c = c * _sigmoid(c)
    mix = _bdot(c, w2_ref[...]) + b2_ref[...]
    o_ref[...] = _layer_norm(ALPHA * x + mix, g_ref[...], be_ref[...])


def _conv_layer(x3, w1, b1, wd, bd, cg, cb, w2, b2, g, be):
    bsz, seq, d = x3.shape
    width = w2.shape[0]
    t = CONV_TILE
    blk = pl.BlockSpec((None, t, d), lambda b, i: (b, i, 0))
    consts = (w1, b1, wd, bd, cg, cb, w2, b2, g, be)
    return pl.pallas_call(
        functools.partial(_conv_kernel, width=width),
        grid=(bsz, seq // t),
        in_specs=[blk] + [_const_spec(c.shape) for c in consts],
        out_specs=blk,
        out_shape=jax.ShapeDtypeStruct((bsz, seq, d), F32),
        scratch_shapes=[pltpu.VMEM((width // LANES, CONV_HIST + t, LANES), F32),
                        pltpu.VMEM((width // LANES, t, LANES), F32)],
        compiler_params=_params("parallel", "arbitrary"),
        name="conformer_conv_ln",
    )(x3, *consts)


def _row(v):
    return v.astype(F32).reshape(1, -1)


def kernel(x, ln1_g, ln1_b, ln2_g, ln2_b, ffn_w1, ffn_b1, ffn_w2, ffn_b2, mix_w_in, mix_b_in, s5_lambda_re, s5_lambda_im, s5_log_dt, s5_b_re, s5_b_im, s5_c_re, s5_c_im, s5_d, s5_w_glu, s5_b_glu, mix_w_out, mix_b_out, conv_w_pw1, conv_b_pw1, conv_w_dw, conv_b_dw, conv_ln_g, conv_ln_b, conv_w_pw2, conv_b_pw2):
    bsz, seq, d = x.shape
    depth = ln1_g.shape[0]
    s5w = s5_d.shape[1]
    sbw = (mix_w_in.shape[2] - s5w) // 3
    n = bsz * seq
    x2 = x.reshape(n, d)
    for layer in range(depth):
        i = layer // 2
        ffn = (ffn_w1[layer].astype(BF16), _row(ffn_b1[layer]),
               ffn_w2[layer].astype(BF16), _row(ffn_b2[layer]),
               _row(ln2_g[layer]), _row(ln2_b[layer]))
        if layer % 2 == 0:
            u, qkv = _in_proj(x2, mix_w_in[i].astype(BF16), _row(mix_b_in[i]), s5w, sbw)
            sbo = _attention(qkv.reshape(bsz, seq, 3 * sbw), sbw)
            bm, cm, coef = _s5_tables(s5_lambda_re[i], s5_lambda_im[i], s5_log_dt[i],
                                      s5_b_re[i], s5_b_im[i], s5_c_re[i], s5_c_im[i])
            s5o = _s5(u.reshape(bsz, seq, s5w), bm, cm, coef, _row(s5_d[i]),
                      s5_w_glu[i].astype(BF16), _row(s5_b_glu[i]))
            x2 = _mix_ffn(x2, s5o.reshape(n, s5w), sbo.reshape(n, sbw),
                          mix_w_out[i].astype(BF16), _row(mix_b_out[i]),
                          _row(ln1_g[layer]), _row(ln1_b[layer]), *ffn)
        else:
            x1 = _conv_layer(x2.reshape(bsz, seq, d), conv_w_pw1[i].astype(BF16),
                             _row(conv_b_pw1[i]), conv_w_dw[i].astype(F32), _row(conv_b_dw[i]),
                             _row(conv_ln_g[i]), _row(conv_ln_b[i]),
                             conv_w_pw2[i].astype(BF16), _row(conv_b_pw2[i]),
                             _row(ln1_g[layer]), _row(ln1_b[layer]))
            x2 = _ffn(x1.reshape(n, d), *ffn)
    return x2.reshape(bsz, seq, d)
```

```python
import functools
import math

import numpy as np
import jax
import jax.numpy as jnp
from jax import lax
from jax.experimental import pallas as pl
from jax.experimental.pallas import tpu as pltpu

F32 = jnp.float32
BF16 = jnp.bfloat16

LANES = 128
SUBLANES = 8
VMEM_LIMIT = 56 * 1024 * 1024

S5_GROUP = 16
S5_STATE = 64
SB_HEAD_DIM = 64
CONV_SIZE = 31
LN_EPS = 1e-5
DEPTH = 2
ALPHA = (2 * DEPTH) ** 0.25

ROW_TILE = 1024
FFN_ROWS = 1024
FFN_CHUNK = 1024
ATT_TILE = 256
ATT_HEADS = 4
ATT_STAGES = 4
ATT_SPLIT = 1
MASK_BIAS = -1e30
EXP2_CLAMP = 126.0
S5_TIME = 64
S5_BLOCK = 16
CONV_TILE = 1024
CONV_HIST = 32


def _params(*sem):
    return pltpu.CompilerParams(dimension_semantics=sem, vmem_limit_bytes=VMEM_LIMIT)


def _const_spec(shape):
    nd = len(shape)
    return pl.BlockSpec(shape, lambda *_: (0,) * nd, pipeline_mode=pl.Buffered(1))


def _layer_norm(r, g, b):
    mu = jnp.mean(r, axis=-1, keepdims=True)
    c = r - mu
    var = jnp.mean(c * c, axis=-1, keepdims=True)
    return c * lax.rsqrt(var + LN_EPS) * g + b


def _bdot(a, w):
    return jnp.dot(a.astype(BF16), w, preferred_element_type=F32)


_sigmoid = jax.nn.sigmoid


def _in_proj_kernel(x_ref, w_ref, b_ref, u_ref, qkv_ref, *, s5w, sbw, scale):
    h = _bdot(x_ref[...], w_ref[...]) + b_ref[...]
    u_ref[...] = h[:, :s5w]
    qkv_ref[:, :sbw] = (h[:, s5w:s5w + sbw] * scale).astype(BF16)
    qkv_ref[:, sbw:] = h[:, s5w + sbw:].astype(BF16)


def _in_proj(x2, w, b, s5w, sbw):
    n, d = x2.shape
    wout = w.shape[1]
    kern = functools.partial(_in_proj_kernel, s5w=s5w, sbw=sbw,
                             scale=SB_HEAD_DIM ** -0.5 * math.log2(math.e))
    return pl.pallas_call(
        kern,
        grid=(n // ROW_TILE,),
        in_specs=[pl.BlockSpec((ROW_TILE, d), lambda i: (i, 0)),
                  _const_spec((d, wout)), _const_spec((1, wout))],
        out_specs=[pl.BlockSpec((ROW_TILE, s5w), lambda i: (i, 0)),
                   pl.BlockSpec((ROW_TILE, 3 * sbw), lambda i: (i, 0))],
        out_shape=[jax.ShapeDtypeStruct((n, s5w), F32),
                   jax.ShapeDtypeStruct((n, 3 * sbw), BF16)],
        compiler_params=_params("parallel"),
        name="in_proj",
    )(x2, w, b)


def _attn_tables(nblk, ncol):
    lead = ATT_STAGES - 1
    tiles = [(0, 0, 2, 0)] * lead
    for col in range(ncol):
        for r in range(nblk):
            tiles += [(r, c, 1 if c == r else 0, col) for c in range(r, -1, -1)]
    steps = -(-len(tiles) // ATT_STAGES) * ATT_STAGES
    tiles += [(0, 0, 2, 0)] * (steps + lead - len(tiles))
    return np.asarray(tiles, np.int32).T.copy(), steps


def _attn_kernel(tab_ref, q_ref, k_ref, v_ref, tri_ref, o_ref,
                 qm_ref, km_ref, vm_ref, z_ref, s_ref, w_ref, bias_ref, acc_ref, *, heads, steps):
    t = ATT_TILE
    width = heads * SB_HEAD_DIM
    seq = q_ref.shape[0]
    ncol = q_ref.shape[1] // width
    nblk = seq // t
    nt = (((1,), (1,)), ((), ()))
    lane = lax.broadcasted_iota(jnp.int32, (1, width), 1)
    masks = [(lane >= h * SB_HEAD_DIM) & (lane < (h + 1) * SB_HEAD_DIM) for h in range(heads)]
    zero = jnp.zeros((), BF16)
    for col in range(ncol):
        cols = slice(col * width, (col + 1) * width)
        for r in range(nblk):
            rows = slice(r * t, (r + 1) * t)
            qr = q_ref[rows, cols]
            vr = v_ref[rows, cols]
            km_ref[col * nblk + r] = k_ref[rows, cols]
            for h in range(heads):
                qm_ref[col * heads + h, rows, :] = jnp.where(masks[h], qr, zero)
                vm_ref[col * nblk + r, h * t:(h + 1) * t, :] = jnp.where(masks[h], vr, zero)
    @pl.when(pl.program_id(0) == 0)
    def _():
        row = lax.broadcasted_iota(jnp.int32, (t, t), 0)
        col = lax.broadcasted_iota(jnp.int32, (t, t), 1)
        bias_ref[0] = jnp.zeros((t, t), F32)
        bias_ref[1] = jnp.where(col < row, 0.0, MASK_BIAS)
        bias_ref[2] = jnp.full((t, t), MASK_BIAS, F32)
        z_ref[...] = jnp.full(z_ref.shape, MASK_BIAS, F32)
        s_ref[...] = jnp.zeros(s_ref.shape, BF16)
        w_ref[...] = jnp.zeros(w_ref.shape, BF16)

    acc_ref[...] = jnp.zeros(acc_ref.shape, F32)
    tri2 = tri_ref[...]

    def substep(n, r, carries):
        slot = r % ATT_STAGES
        rows = pl.ds(pl.multiple_of(tab_ref[0, n] * t, t), t)
        col = tab_ref[3, n]
        acc_ref[col, rows, :] += jnp.dot(w_ref[slot], vm_ref[col * nblk + tab_ref[1, n]],
                                         preferred_element_type=F32)
        slot = (r + 1) % ATT_STAGES
        row_start = tab_ref[2, n + 1] == 1
        new_carries = []
        for h in range(heads):
            c = jnp.where(row_start, 0.0, carries[h])
            incl = jnp.dot(s_ref[slot, h], tri2, preferred_element_type=F32)
            w_ref[slot, :, h * t:(h + 1) * t] = jnp.exp2((z_ref[slot, h] - c) - incl).astype(BF16)
            new_carries.append(c + incl[:, 0:1])
        slot = (r + 2) % ATT_STAGES
        for h in range(heads):
            z = z_ref[slot, h]
            sp = jnp.maximum(z, jnp.log2(1.0 + jnp.exp2(jnp.minimum(z, EXP2_CLAMP))))
            for p in range(ATT_SPLIT):
                part = sp.astype(BF16)
                s_ref[slot, h, :, p * t:(p + 1) * t] = part
                if p + 1 < ATT_SPLIT:
                    sp = sp - part.astype(F32)
        slot = (r + 3) % ATT_STAGES
        col = tab_ref[3, n + 3]
        kj = km_ref[col * nblk + tab_ref[1, n + 3]]
        bias = bias_ref[tab_ref[2, n + 3]]
        qrows = pl.ds(pl.multiple_of(tab_ref[0, n + 3] * t, t), t)
        for h in range(heads):
            z_ref[slot, h] = lax.dot_general(qm_ref[col * heads + h, qrows, :], kj, nt,
                                             preferred_element_type=F32) + bias
        return tuple(new_carries)

    def body(it, carries):
        for r in range(ATT_STAGES):
            carries = substep(ATT_STAGES * it + r, r, carries)
        return carries

    lax.fori_loop(0, steps // ATT_STAGES, body,
                  tuple(jnp.zeros((t, 1), F32) for _ in range(heads)))
    for col in range(ncol):
        o_ref[:, col * width:(col + 1) * width] = acc_ref[col].astype(BF16)


def _attention(qkv3, sbw):
    bsz, seq, _ = qkv3.shape
    width = ATT_HEADS * SB_HEAD_DIM
    ncol = sbw // width
    t = ATT_TILE
    nblk = seq // t
    table, steps = _attn_tables(nblk, ncol)
    tri = (lax.broadcasted_iota(jnp.int32, (ATT_SPLIT * t, t), 0) % t
           >= lax.broadcasted_iota(jnp.int32, (ATT_SPLIT * t, t), 1)).astype(BF16)
    grid_spec = pltpu.PrefetchScalarGridSpec(
        num_scalar_prefetch=1,
        grid=(bsz,),
        in_specs=[pl.BlockSpec((None, seq, sbw), lambda b, tab: (b, 0, 0)),
                  pl.BlockSpec((None, seq, sbw), lambda b, tab: (b, 0, 1)),
                  pl.BlockSpec((None, seq, sbw), lambda b, tab: (b, 0, 2)),
                  _const_spec((ATT_SPLIT * t, t))],
        out_specs=pl.BlockSpec((None, seq, sbw), lambda b, tab: (b, 0, 0)),
        scratch_shapes=[pltpu.VMEM((ncol * ATT_HEADS, seq, width), BF16),
                        pltpu.VMEM((ncol * nblk, t, width), BF16),
                        pltpu.VMEM((ncol * nblk, ATT_HEADS * t, width), BF16),
                        pltpu.VMEM((ATT_STAGES, ATT_HEADS, t, t), F32),
                        pltpu.VMEM((ATT_STAGES, ATT_HEADS, t, ATT_SPLIT * t), BF16),
                        pltpu.VMEM((ATT_STAGES, t, ATT_HEADS * t), BF16),
                        pltpu.VMEM((3, t, t), F32),
                        pltpu.VMEM((ncol, seq, width), F32)])
    return pl.pallas_call(
        functools.partial(_attn_kernel, heads=ATT_HEADS, steps=steps),
        grid_spec=grid_spec,
        out_shape=jax.ShapeDtypeStruct((bsz, seq, sbw), BF16),
        compiler_params=_params("arbitrary"),
        name="stickbreak_attn",
    )(jnp.asarray(table), qkv3, qkv3, qkv3, tri)


def _s5_tables(lam_re, lam_im, log_dt, b_re, b_im, c_re, c_im):
    g, n = lam_re.shape
    p = b_re.shape[-1]
    gh = g // 2
    npair = gh // 2
    lr = jnp.minimum(lam_re.astype(F32), -1e-4)
    li = lam_im.astype(F32)
    dt = jnp.exp(log_dt.astype(F32))[:, None]
    mag = jnp.exp(lr * dt)
    lb_re = mag * jnp.cos(li * dt)
    lb_im = mag * jnp.sin(li * dt)
    den = lr * lr + li * li
    nr, ni = lb_re - 1.0, lb_im
    cf_re = (nr * lr + ni * li) / den
    cf_im = (ni * lr - nr * li) / den
    bb_re = cf_re[..., None] * b_re - cf_im[..., None] * b_im
    bb_im = cf_re[..., None] * b_im + cf_im[..., None] * b_re
    eye_par = jnp.eye(2, dtype=F32)
    eye_pair = jnp.eye(npair, dtype=F32)

    tb = jnp.stack([bb_re, bb_im], 0).transpose(1, 3, 0, 2)
    tb = tb.reshape(2, npair, 2, p, 2, n)
    tb = tb[..., None, :] * eye_par[None, None, :, None, None, :, None]
    tb = tb.reshape(2, npair, 2 * p, 4 * n)
    bm = (tb[:, :, :, None, :] * eye_pair[None, :, None, :, None]).reshape(2, gh * p, gh * 2 * n)

    tc = jnp.stack([c_re, -c_im], 0).astype(F32)
    tc = tc.reshape(2, 2, npair, 2, p, n).transpose(1, 2, 0, 3, 5, 4)
    tc = tc[..., None, :] * eye_par[None, None, None, :, None, :, None]
    tc = tc.reshape(2, npair, 4 * n, 2 * p)
    cm = (tc[:, :, :, None, :] * eye_pair[None, :, None, :, None]).reshape(2, gh * 2 * n, gh * p)

    def cols(v):
        v = v.reshape(2, npair, 1, 2, n)
        return jnp.broadcast_to(v, (2, npair, 2, 2, n)).reshape(1, -1)

    coef = jnp.concatenate([jnp.broadcast_to(cols(lb_re), (SUBLANES, 2 * g * n)),
                            jnp.broadcast_to(cols(lb_im), (SUBLANES, 2 * g * n))], 0)
    return bm.astype(BF16), cm.astype(BF16), coef


def _gelu_tanh(x):
    return 0.5 * x * (1.0 + jnp.tanh(math.sqrt(2.0 / math.pi) * (x + 0.044715 * (x * x * x))))


def _s5_kernel(u_ref, bm_ref, cm_ref, coef_ref, d_ref, wg_ref, bg_ref, o_ref,
               utb_ref, bu_ref, st_ref, h_ref, otb_ref, *, nb, tt, width):
    nslab = width // LANES
    ncols = st_ref.shape[1]
    half = ncols // 2
    tiles = nb // SUBLANES

    @pl.when(pl.program_id(0) == 0)
    def _():
        h_ref[...] = jnp.zeros_like(h_ref)

    def seq_rows(b):
        return pl.ds((b // SUBLANES) * tt * SUBLANES + b % SUBLANES, tt, stride=SUBLANES)

    for b in range(nb):
        for s in range(nslab):
            utb_ref[s, seq_rows(b), :] = u_ref[b, :, s * LANES:(s + 1) * LANES]

    hw = width // 2
    nblock = tt // S5_BLOCK

    def block_rows(k):
        return [slice((s * tt + k * S5_BLOCK) * SUBLANES, (s * tt + (k + 1) * S5_BLOCK) * SUBLANES)
                for s in range(tiles)]

    def block_u(k):
        return jnp.concatenate(
            [jnp.concatenate([utb_ref[s, rs, :] for s in range(nslab)], axis=1)
             for rs in block_rows(k)], axis=0)

    def drive(k):
        ub = block_u(k)
        seg = S5_BLOCK * SUBLANES
        for hf in range(2):
            bu = _bdot(ub[:, hf * hw:(hf + 1) * hw], bm_ref[hf])
            for i, rs in enumerate(block_rows(k)):
                bu_ref[rs, hf * half:(hf + 1) * half] = bu[i * seg:(i + 1) * seg, :]

    def scan(k):
        pairs = 4
        for cg in range(ncols // (2 * LANES * pairs)):
            base = cg * 2 * LANES * pairs
            re_c = [base + q * 2 * LANES for q in range(pairs)]
            im_c = [c + LANES for c in re_c]
            a_re = [coef_ref[0:SUBLANES, c:c + LANES] for c in re_c]
            a_im = [coef_ref[SUBLANES:2 * SUBLANES, c:c + LANES] for c in re_c]
            h = [[h_ref[s * SUBLANES:(s + 1) * SUBLANES, c:c + LANES] for s in range(tiles)]
                 for q in range(pairs) for c in (re_c[q], im_c[q])]
            for t in range(k * S5_BLOCK, (k + 1) * S5_BLOCK):
                for q in range(pairs):
                    for s in range(tiles):
                        rs = slice((s * tt + t) * SUBLANES, (s * tt + t + 1) * SUBLANES)
                        hr, hi = h[2 * q][s], h[2 * q + 1][s]
                        nr = a_re[q] * hr - a_im[q] * hi + bu_ref[rs, re_c[q]:re_c[q] + LANES]
                        ni = a_re[q] * hi + a_im[q] * hr + bu_ref[rs, im_c[q]:im_c[q] + LANES]
                        st_ref[rs, re_c[q]:re_c[q] + LANES] = nr
                        st_ref[rs, im_c[q]:im_c[q] + LANES] = ni
                        h[2 * q][s], h[2 * q + 1][s] = nr, ni
            for q in range(pairs):
                for part, c in enumerate((re_c[q], im_c[q])):
                    for s in range(tiles):
                        h_ref[s * SUBLANES:(s + 1) * SUBLANES, c:c + LANES] = h[2 * q + part][s]

    def readout(k):
        st = jnp.concatenate([st_ref[rs, :] for rs in block_rows(k)], axis=0)
        y = jnp.concatenate(
            [_bdot(st[:, hf * half:(hf + 1) * half], cm_ref[hf]) for hf in range(2)], axis=1)
        y = _gelu_tanh(y + d_ref[...] * block_u(k))
        hg = _bdot(y, wg_ref[...]) + bg_ref[...]
        out = hg[:, :width] * _sigmoid(hg[:, width:])
        seg = S5_BLOCK * SUBLANES
        for i, rs in enumerate(block_rows(k)):
            for s in range(nslab):
                otb_ref[s, rs, :] = out[i * seg:(i + 1) * seg, s * LANES:(s + 1) * LANES]

    drive(0)
    for k in range(nblock):
        if k + 1 < nblock:
            drive(k + 1)
        scan(k)
        if k >= 1:
            readout(k - 1)
    readout(nblock - 1)

    for b in range(nb):
        o_ref[b] = jnp.concatenate(
            [otb_ref[s, seq_rows(b), :] for s in range(nslab)], axis=1).astype(BF16)


def _s5(u3, bm, cm, coef, d, wg, bg):
    nb, seq, width = u3.shape
    tt = S5_TIME
    ncols = coef.shape[1]
    kern = functools.partial(_s5_kernel, nb=nb, tt=tt, width=width)
    return pl.pallas_call(
        kern,
        grid=(seq // tt,),
        in_specs=[pl.BlockSpec((nb, tt, width), lambda i: (0, i, 0)),
                  _const_spec(bm.shape), _const_spec(cm.shape), _const_spec(coef.shape),
                  _const_spec(d.shape), _const_spec(wg.shape), _const_spec(bg.shape)],
        out_specs=pl.BlockSpec((nb, tt, width), lambda i: (0, i, 0)),
        out_shape=jax.ShapeDtypeStruct((nb, seq, width), BF16),
        scratch_shapes=[pltpu.VMEM((width // LANES, tt * nb, LANES), F32),
                        pltpu.VMEM((tt * nb, ncols), F32),
                        pltpu.VMEM((tt * nb, ncols), F32),
                        pltpu.VMEM((nb, ncols), F32),
                        pltpu.VMEM((width // LANES, tt * nb, LANES), F32)],
        compiler_params=_params("arbitrary"),
        name="s5_mixer",
    )(u3, bm, cm, coef, d, wg, bg)


def _ffn_tail(x, w1_ref, b1_ref, w2_ref, b2_ref, g_ref, be_ref):
    xb = x.astype(BF16)
    acc = ALPHA * x + b2_ref[...]
    for c in range(w1_ref.shape[1] // FFN_CHUNK):
        cs = slice(c * FFN_CHUNK, (c + 1) * FFN_CHUNK)
        h = jnp.dot(xb, w1_ref[:, cs], preferred_element_type=F32) + b1_ref[:, cs]
        h = jnp.square(jnp.maximum(h, 0.0))
        acc = acc + jnp.dot(h.astype(BF16), w2_ref[cs, :], preferred_element_type=F32)
    return _layer_norm(acc, g_ref[...], be_ref[...])


def _ffn_kernel(x_ref, w1_ref, b1_ref, w2_ref, b2_ref, g_ref, be_ref, o_ref):
    o_ref[...] = _ffn_tail(x_ref[...], w1_ref, b1_ref, w2_ref, b2_ref, g_ref, be_ref)


def _mix_ffn_kernel(x_ref, a_ref, s_ref, wo_ref, bo_ref, g1_ref, be1_ref,
                    w1_ref, b1_ref, w2_ref, b2_ref, g2_ref, be2_ref, o_ref, *, s5w):
    mix = (jnp.dot(a_ref[...], wo_ref[:s5w, :], preferred_element_type=F32)
           + jnp.dot(s_ref[...], wo_ref[s5w:, :], preferred_element_type=F32) + bo_ref[...])
    x1 = _layer_norm(ALPHA * x_ref[...] + mix, g1_ref[...], be1_ref[...])
    o_ref[...] = _ffn_tail(x1, w1_ref, b1_ref, w2_ref, b2_ref, g2_ref, be2_ref)


def _row_spec(width):
    return pl.BlockSpec((FFN_ROWS, width), lambda i: (i, 0))


def _ffn(x2, w1, b1, w2, b2, g, be):
    n, d = x2.shape
    consts = (w1, b1, w2, b2, g, be)
    return pl.pallas_call(
        _ffn_kernel,
        grid=(n // FFN_ROWS,),
        in_specs=[_row_spec(d)] + [_const_spec(c.shape) for c in consts],
        out_specs=_row_spec(d),
        out_shape=jax.ShapeDtypeStruct((n, d), F32),
        compiler_params=_params("parallel"),
        name="ffn_ln",
    )(x2, *consts)


def _mix_ffn(x2, s5o, sbo, wo, bo, g1, be1, w1, b1, w2, b2, g2, be2):
    n, d = x2.shape
    s5w, sbw = s5o.shape[1], sbo.shape[1]
    consts = (wo, bo, g1, be1, w1, b1, w2, b2, g2, be2)
    return pl.pallas_call(
        functools.partial(_mix_ffn_kernel, s5w=s5w),
        grid=(n // FFN_ROWS,),
        in_specs=[_row_spec(d), _row_spec(s5w), _row_spec(sbw)]
                 + [_const_spec(c.shape) for c in consts],
        out_specs=_row_spec(d),
        out_shape=jax.ShapeDtypeStruct((n, d), F32),
        compiler_params=_params("parallel"),
        name="out_proj_ffn_ln",
    )(x2, s5o, sbo, *consts)


def _conv_kernel(x_ref, w1_ref, b1_ref, wd_ref, bd_ref, cg_ref, cb_ref, w2_ref, b2_ref,
                 g_ref, be_ref, o_ref, buf_ref, cv_ref, *, width):
    t = CONV_TILE
    nslab = width // LANES

    @pl.when(pl.program_id(1) == 0)
    def _():
        buf_ref[:, 0:CONV_HIST, :] = jnp.zeros((nslab, CONV_HIST, LANES), F32)

    x = x_ref[...]
    h = _bdot(x, w1_ref[...]) + b1_ref[...]
    glu = h[:, :width] * _sigmoid(h[:, width:])
    for cb in range(nslab):
        buf_ref[cb, CONV_HIST:CONV_HIST + t, :] = glu[:, cb * LANES:(cb + 1) * LANES]

    first = CONV_HIST - (CONV_SIZE - 1)
    for cb in range(nslab):
        cs = slice(cb * LANES, (cb + 1) * LANES)
        for par in range(2):
            acc = jnp.broadcast_to(bd_ref[:, cs], (t // 2, LANES))
            for k in range(CONV_SIZE):
                acc = acc + wd_ref[k:k + 1, cs] * buf_ref[cb, pl.ds(first + k + par, t // 2, stride=2), :]
            cv_ref[cb, pl.ds(par, t // 2, stride=2), :] = acc
        buf_ref[cb, 0:CONV_HIST, :] = buf_ref[cb, t:t + CONV_HIST, :]

    conv = jnp.concatenate([cv_ref[cb] for cb in range(nslab)], axis=1)
    c = _layer_norm(conv, cg_ref[...], cb_ref[...])
    c = c * _sigmoid(c)
    mix = _bdot(c, w2_ref[...]) + b2_ref[...]
    o_ref[...] = _layer_norm(ALPHA * x + mix, g_ref[...], be_ref[...])


def _conv_layer(x3, w1, b1, wd, bd, cg, cb, w2, b2, g, be):
    bsz, seq, d = x3.shape
    width = w2.shape[0]
    t = CONV_TILE
    blk = pl.BlockSpec((None, t, d), lambda b, i: (b, i, 0))
    consts = (w1, b1, wd, bd, cg, cb, w2, b2, g, be)
    return pl.pallas_call(
        functools.partial(_conv_kernel, width=width),
        grid=(bsz, seq // t),
        in_specs=[blk] + [_const_spec(c.shape) for c in consts],
        out_specs=blk,
        out_shape=jax.ShapeDtypeStruct((bsz, seq, d), F32),
        scratch_shapes=[pltpu.VMEM((width // LANES, CONV_HIST + t, LANES), F32),
                        pltpu.VMEM((width // LANES, t, LANES), F32)],
        compiler_params=_params("parallel", "arbitrary"),
        name="conformer_conv_ln",
    )(x3, *consts)


def _row(v):
    return v.astype(F32).reshape(1, -1)


def kernel(x, ln1_g, ln1_b, ln2_g, ln2_b, ffn_w1, ffn_b1, ffn_w2, ffn_b2, mix_w_in, mix_b_in, s5_lambda_re, s5_lambda_im, s5_log_dt, s5_b_re, s5_b_im, s5_c_re, s5_c_im, s5_d, s5_w_glu, s5_b_glu, mix_w_out, mix_b_out, conv_w_pw1, conv_b_pw1, conv_w_dw, conv_b_dw, conv_ln_g, conv_ln_b, conv_w_pw2, conv_b_pw2):
    bsz, seq, d = x.shape
    depth = ln1_g.shape[0]
    s5w = s5_d.shape[1]
    sbw = (mix_w_in.shape[2] - s5w) // 3
    n = bsz * seq
    x2 = x.reshape(n, d)
    for layer in range(depth):
        i = layer // 2
        ffn = (ffn_w1[layer].astype(BF16), _row(ffn_b1[layer]),
               ffn_w2[layer].astype(BF16), _row(ffn_b2[layer]),
               _row(ln2_g[layer]), _row(ln2_b[layer]))
        if layer % 2 == 0:
            u, qkv = _in_proj(x2, mix_w_in[i].astype(BF16), _row(mix_b_in[i]), s5w, sbw)
            sbo = _attention(qkv.reshape(bsz, seq, 3 * sbw), sbw)
            bm, cm, coef = _s5_tables(s5_lambda_re[i], s5_lambda_im[i], s5_log_dt[i],
                                      s5_b_re[i], s5_b_im[i], s5_c_re[i], s5_c_im[i])
            s5o = _s5(u.reshape(bsz, seq, s5w), bm, cm, coef, _row(s5_d[i]),
                      s5_w_glu[i].astype(BF16), _row(s5_b_glu[i]))
            x2 = _mix_ffn(x2, s5o.reshape(n, s5w), sbo.reshape(n, sbw),
                          mix_w_out[i].astype(BF16), _row(mix_b_out[i]),
                          _row(ln1_g[layer]), _row(ln1_b[layer]), *ffn)
        else:
            x1 = _conv_layer(x2.reshape(bsz, seq, d), conv_w_pw1[i].astype(BF16),
                             _row(conv_b_pw1[i]), conv_w_dw[i].astype(F32), _row(conv_b_dw[i]),
                             _row(conv_ln_g[i]), _row(conv_ln_b[i]),
                             conv_w_pw2[i].astype(BF16), _row(conv_b_pw2[i]),
                             _row(ln1_g[layer]), _row(ln1_b[layer]))
            x2 = _ffn(x1.reshape(n, d), *ffn)
    return x2.reshape(bsz, seq, d)
```

```python
import functools
import math

import numpy as np
import jax
import jax.numpy as jnp
from jax import lax
from jax.experimental import pallas as pl
from jax.experimental.pallas import tpu as pltpu

F32 = jnp.float32
BF16 = jnp.bfloat16

LANES = 128
SUBLANES = 8
VMEM_LIMIT = 56 * 1024 * 1024

S5_GROUP = 16
S5_STATE = 64
SB_HEAD_DIM = 64
CONV_SIZE = 31
LN_EPS = 1e-5
DEPTH = 2
ALPHA = (2 * DEPTH) ** 0.25

ROW_TILE = 1024
FFN_ROWS = 1024
FFN_CHUNK = 1024
ATT_TILE = 256
ATT_HEADS = 4
ATT_STAGES = 4
ATT_SPLIT = 1
MASK_BIAS = -1e30
EXP2_CLAMP = 126.0
S5_TIME = 64
S5_BLOCK = 32
CONV_TILE = 1024
CONV_HIST = 32


def _params(*sem):
    return pltpu.CompilerParams(dimension_semantics=sem, vmem_limit_bytes=VMEM_LIMIT)


def _const_spec(shape):
    nd = len(shape)
    return pl.BlockSpec(shape, lambda *_: (0,) * nd, pipeline_mode=pl.Buffered(1))


def _layer_norm(r, g, b):
    mu = jnp.mean(r, axis=-1, keepdims=True)
    c = r - mu
    var = jnp.mean(c * c, axis=-1, keepdims=True)
    return c * lax.rsqrt(var + LN_EPS) * g + b


def _bdot(a, w):
    return jnp.dot(a.astype(BF16), w, preferred_element_type=F32)


_sigmoid = jax.nn.sigmoid


def _in_proj_kernel(x_ref, w_ref, b_ref, u_ref, qkv_ref, *, s5w, sbw, scale):
    h = _bdot(x_ref[...], w_ref[...]) + b_ref[...]
    u_ref[...] = h[:, :s5w]
    qkv_ref[:, :sbw] = (h[:, s5w:s5w + sbw] * scale).astype(BF16)
    qkv_ref[:, sbw:] = h[:, s5w + sbw:].astype(BF16)


def _in_proj(x2, w, b, s5w, sbw):
    n, d = x2.shape
    wout = w.shape[1]
    kern = functools.partial(_in_proj_kernel, s5w=s5w, sbw=sbw,
                             scale=SB_HEAD_DIM ** -0.5 * math.log2(math.e))
    return pl.pallas_call(
        kern,
        grid=(n // ROW_TILE,),
        in_specs=[pl.BlockSpec((ROW_TILE, d), lambda i: (i, 0)),
                  _const_spec((d, wout)), _const_spec((1, wout))],
        out_specs=[pl.BlockSpec((ROW_TILE, s5w), lambda i: (i, 0)),
                   pl.BlockSpec((ROW_TILE, 3 * sbw), lambda i: (i, 0))],
        out_shape=[jax.ShapeDtypeStruct((n, s5w), F32),
                   jax.ShapeDtypeStruct((n, 3 * sbw), BF16)],
        compiler_params=_params("parallel"),
        name="in_proj",
    )(x2, w, b)


def _attn_tables(nblk, ncol):
    lead = ATT_STAGES - 1
    tiles = [(0, 0, 2, 0)] * lead
    for col in range(ncol):
        for r in range(nblk):
            tiles += [(r, c, 1 if c == r else 0, col) for c in range(r, -1, -1)]
    steps = -(-len(tiles) // ATT_STAGES) * ATT_STAGES
    tiles += [(0, 0, 2, 0)] * (steps + lead - len(tiles))
    return np.asarray(tiles, np.int32).T.copy(), steps


def _attn_kernel(tab_ref, q_ref, k_ref, v_ref, tri_ref, o_ref,
                 qm_ref, km_ref, vm_ref, z_ref, s_ref, w_ref, bias_ref, acc_ref, *, heads, steps):
    t = ATT_TILE
    width = heads * SB_HEAD_DIM
    seq = q_ref.shape[0]
    ncol = q_ref.shape[1] // width
    nblk = seq // t
    nt = (((1,), (1,)), ((), ()))
    lane = lax.broadcasted_iota(jnp.int32, (1, width), 1)
    masks = [(lane >= h * SB_HEAD_DIM) & (lane < (h + 1) * SB_HEAD_DIM) for h in range(heads)]
    zero = jnp.zeros((), BF16)
    for col in range(ncol):
        cols = slice(col * width, (col + 1) * width)
        for r in range(nblk):
            rows = slice(r * t, (r + 1) * t)
            qr = q_ref[rows, cols]
            vr = v_ref[rows, cols]
            km_ref[col * nblk + r] = k_ref[rows, cols]
            for h in range(heads):
                qm_ref[col * heads + h, rows, :] = jnp.where(masks[h], qr, zero)
                vm_ref[col * nblk + r, h * t:(h + 1) * t, :] = jnp.where(masks[h], vr, zero)
    @pl.when(pl.program_id(0) == 0)
    def _():
        row = lax.broadcasted_iota(jnp.int32, (t, t), 0)
        col = lax.broadcasted_iota(jnp.int32, (t, t), 1)
        bias_ref[0] = jnp.zeros((t, t), F32)
        bias_ref[1] = jnp.where(col < row, 0.0, MASK_BIAS)
        bias_ref[2] = jnp.full((t, t), MASK_BIAS, F32)
        z_ref[...] = jnp.full(z_ref.shape, MASK_BIAS, F32)
        s_ref[...] = jnp.zeros(s_ref.shape, BF16)
        w_ref[...] = jnp.zeros(w_ref.shape, BF16)

    acc_ref[...] = jnp.zeros(acc_ref.shape, F32)
    tri2 = tri_ref[...]

    def substep(n, r, carries):
        slot = r % ATT_STAGES
        rows = pl.ds(pl.multiple_of(tab_ref[0, n] * t, t), t)
        col = tab_ref[3, n]
        acc_ref[col, rows, :] += jnp.dot(w_ref[slot], vm_ref[col * nblk + tab_ref[1, n]],
                                         preferred_element_type=F32)
        slot = (r + 1) % ATT_STAGES
        row_start = tab_ref[2, n + 1] == 1
        new_carries = []
        for h in range(heads):
            c = jnp.where(row_start, 0.0, carries[h])
            incl = jnp.dot(s_ref[slot, h], tri2, preferred_element_type=F32)
            w_ref[slot, :, h * t:(h + 1) * t] = jnp.exp2((z_ref[slot, h] - c) - incl).astype(BF16)
            new_carries.append(c + incl[:, 0:1])
        slot = (r + 2) % ATT_STAGES
        for h in range(heads):
            z = z_ref[slot, h]
            sp = jnp.maximum(z, jnp.log2(1.0 + jnp.exp2(jnp.minimum(z, EXP2_CLAMP))))
            for p in range(ATT_SPLIT):
                part = sp.astype(BF16)
                s_ref[slot, h, :, p * t:(p + 1) * t] = part
                if p + 1 < ATT_SPLIT:
                    sp = sp - part.astype(F32)
        slot = (r + 3) % ATT_STAGES
        col = tab_ref[3, n + 3]
        kj = km_ref[col * nblk + tab_ref[1, n + 3]]
        bias = bias_ref[tab_ref[2, n + 3]]
        qrows = pl.ds(pl.multiple_of(tab_ref[0, n + 3] * t, t), t)
        for h in range(heads):
            z_ref[slot, h] = lax.dot_general(qm_ref[col * heads + h, qrows, :], kj, nt,
                                             preferred_element_type=F32) + bias
        return tuple(new_carries)

    def body(it, carries):
        for r in range(ATT_STAGES):
            carries = substep(ATT_STAGES * it + r, r, carries)
        return carries

    lax.fori_loop(0, steps // ATT_STAGES, body,
                  tuple(jnp.zeros((t, 1), F32) for _ in range(heads)))
    for col in range(ncol):
        o_ref[:, col * width:(col + 1) * width] = acc_ref[col].astype(BF16)


def _attention(qkv3, sbw):
    bsz, seq, _ = qkv3.shape
    width = ATT_HEADS * SB_HEAD_DIM
    ncol = sbw // width
    t = ATT_TILE
    nblk = seq // t
    table, steps = _attn_tables(nblk, ncol)
    tri = (lax.broadcasted_iota(jnp.int32, (ATT_SPLIT * t, t), 0) % t
           >= lax.broadcasted_iota(jnp.int32, (ATT_SPLIT * t, t), 1)).astype(BF16)
    grid_spec = pltpu.PrefetchScalarGridSpec(
        num_scalar_prefetch=1,
        grid=(bsz,),
        in_specs=[pl.BlockSpec((None, seq, sbw), lambda b, tab: (b, 0, 0)),
                  pl.BlockSpec((None, seq, sbw), lambda b, tab: (b, 0, 1)),
                  pl.BlockSpec((None, seq, sbw), lambda b, tab: (b, 0, 2)),
                  _const_spec((ATT_SPLIT * t, t))],
        out_specs=pl.BlockSpec((None, seq, sbw), lambda b, tab: (b, 0, 0)),
        scratch_shapes=[pltpu.VMEM((ncol * ATT_HEADS, seq, width), BF16),
                        pltpu.VMEM((ncol * nblk, t, width), BF16),
                        pltpu.VMEM((ncol * nblk, ATT_HEADS * t, width), BF16),
                        pltpu.VMEM((ATT_STAGES, ATT_HEADS, t, t), F32),
                        pltpu.VMEM((ATT_STAGES, ATT_HEADS, t, ATT_SPLIT * t), BF16),
                        pltpu.VMEM((ATT_STAGES, t, ATT_HEADS * t), BF16),
                        pltpu.VMEM((3, t, t), F32),
                        pltpu.VMEM((ncol, seq, width), F32)])
    return pl.pallas_call(
        functools.partial(_attn_kernel, heads=ATT_HEADS, steps=steps),
        grid_spec=grid_spec,
        out_shape=jax.ShapeDtypeStruct((bsz, seq, sbw), BF16),
        compiler_params=_params("arbitrary"),
        name="stickbreak_attn",
    )(jnp.asarray(table), qkv3, qkv3, qkv3, tri)


def _s5_tables(lam_re, lam_im, log_dt, b_re, b_im, c_re, c_im):
    g, n = lam_re.shape
    p = b_re.shape[-1]
    gh = g // 2
    npair = gh // 2
    lr = jnp.minimum(lam_re.astype(F32), -1e-4)
    li = lam_im.astype(F32)
    dt = jnp.exp(log_dt.astype(F32))[:, None]
    mag = jnp.exp(lr * dt)
    lb_re = mag * jnp.cos(li * dt)
    lb_im = mag * jnp.sin(li * dt)
    den = lr * lr + li * li
    nr, ni = lb_re - 1.0, lb_im
    cf_re = (nr * lr + ni * li) / den
    cf_im = (ni * lr - nr * li) / den
    bb_re = cf_re[..., None] * b_re - cf_im[..., None] * b_im
    bb_im = cf_re[..., None] * b_im + cf_im[..., None] * b_re
    eye_par = jnp.eye(2, dtype=F32)
    eye_pair = jnp.eye(npair, dtype=F32)

    tb = jnp.stack([bb_re, bb_im], 0).transpose(1, 3, 0, 2)
    tb = tb.reshape(2, npair, 2, p, 2, n)
    tb = tb[..., None, :] * eye_par[None, None, :, None, None, :, None]
    tb = tb.reshape(2, npair, 2 * p, 4 * n)
    bm = (tb[:, :, :, None, :] * eye_pair[None, :, None, :, None]).reshape(2, gh * p, gh * 2 * n)

    tc = jnp.stack([c_re, -c_im], 0).astype(F32)
    tc = tc.reshape(2, 2, npair, 2, p, n).transpose(1, 2, 0, 3, 5, 4)
    tc = tc[..., None, :] * eye_par[None, None, None, :, None, :, None]
    tc = tc.reshape(2, npair, 4 * n, 2 * p)
    cm = (tc[:, :, :, None, :] * eye_pair[None, :, None, :, None]).reshape(2, gh * 2 * n, gh * p)

    def cols(v):
        v = v.reshape(2, npair, 1, 2, n)
        return jnp.broadcast_to(v, (2, npair, 2, 2, n)).reshape(1, -1)

    coef = jnp.concatenate([jnp.broadcast_to(cols(lb_re), (SUBLANES, 2 * g * n)),
                            jnp.broadcast_to(cols(lb_im), (SUBLANES, 2 * g * n))], 0)
    return bm.astype(BF16), cm.astype(BF16), coef


def _gelu_tanh(x):
    return 0.5 * x * (1.0 + jnp.tanh(math.sqrt(2.0 / math.pi) * (x + 0.044715 * (x * x * x))))


def _s5_kernel(u_ref, bm_ref, cm_ref, coef_ref, d_ref, wg_ref, bg_ref, o_ref,
               utb_ref, bu_ref, st_ref, h_ref, otb_ref, *, nb, tt, width):
    nslab = width // LANES
    ncols = st_ref.shape[1]
    half = ncols // 2
    tiles = nb // SUBLANES

    @pl.when(pl.program_id(0) == 0)
    def _():
        h_ref[...] = jnp.zeros_like(h_ref)

    def seq_rows(b):
        return pl.ds((b // SUBLANES) * tt * SUBLANES + b % SUBLANES, tt, stride=SUBLANES)

    for b in range(nb):
        for s in range(nslab):
            utb_ref[s, seq_rows(b), :] = u_ref[b, :, s * LANES:(s + 1) * LANES]

    hw = width // 2
    nblock = tt // S5_BLOCK

    def block_rows(k):
        return [slice((s * tt + k * S5_BLOCK) * SUBLANES, (s * tt + (k + 1) * S5_BLOCK) * SUBLANES)
                for s in range(tiles)]

    def block_u(k):
        return jnp.concatenate(
            [jnp.concatenate([utb_ref[s, rs, :] for s in range(nslab)], axis=1)
             for rs in block_rows(k)], axis=0)

    def drive(k):
        ub = block_u(k)
        seg = S5_BLOCK * SUBLANES
        for hf in range(2):
            bu = _bdot(ub[:, hf * hw:(hf + 1) * hw], bm_ref[hf])
            for i, rs in enumerate(block_rows(k)):
                bu_ref[rs, hf * half:(hf + 1) * half] = bu[i * seg:(i + 1) * seg, :]

    def scan(k):
        pairs = 4
        for cg in range(ncols // (2 * LANES * pairs)):
            base = cg * 2 * LANES * pairs
            re_c = [base + q * 2 * LANES for q in range(pairs)]
            im_c = [c + LANES for c in re_c]
            a_re = [coef_ref[0:SUBLANES, c:c + LANES] for c in re_c]
            a_im = [coef_ref[SUBLANES:2 * SUBLANES, c:c + LANES] for c in re_c]
            h = [[h_ref[s * SUBLANES:(s + 1) * SUBLANES, c:c + LANES] for s in range(tiles)]
                 for q in range(pairs) for c in (re_c[q], im_c[q])]
            for t in range(k * S5_BLOCK, (k + 1) * S5_BLOCK):
                for q in range(pairs):
                    for s in range(tiles):
                        rs = slice((s * tt + t) * SUBLANES, (s * tt + t + 1) * SUBLANES)
                        hr, hi = h[2 * q][s], h[2 * q + 1][s]
                        nr = a_re[q] * hr - a_im[q] * hi + bu_ref[rs, re_c[q]:re_c[q] + LANES]
                        ni = a_re[q] * hi + a_im[q] * hr + bu_ref[rs, im_c[q]:im_c[q] + LANES]
                        st_ref[rs, re_c[q]:re_c[q] + LANES] = nr
                        st_ref[rs, im_c[q]:im_c[q] + LANES] = ni
                        h[2 * q][s], h[2 * q + 1][s] = nr, ni
            for q in range(pairs):
                for part, c in enumerate((re_c[q], im_c[q])):
                    for s in range(tiles):
                        h_ref[s * SUBLANES:(s + 1) * SUBLANES, c:c + LANES] = h[2 * q + part][s]

    def readout(k):
        st = jnp.concatenate([st_ref[rs, :] for rs in block_rows(k)], axis=0)
        y = jnp.concatenate(
            [_bdot(st[:, hf * half:(hf + 1) * half], cm_ref[hf]) for hf in range(2)], axis=1)
        y = _gelu_tanh(y + d_ref[...] * block_u(k))
        hg = _bdot(y, wg_ref[...]) + bg_ref[...]
        out = hg[:, :width] * _sigmoid(hg[:, width:])
        seg = S5_BLOCK * SUBLANES
        for i, rs in enumerate(block_rows(k)):
            for s in range(nslab):
                otb_ref[s, rs, :] = out[i * seg:(i + 1) * seg, s * LANES:(s + 1) * LANES]

    drive(0)
    for k in range(nblock):
        if k + 1 < nblock:
            drive(k + 1)
        scan(k)
        if k >= 1:
            readout(k - 1)
    readout(nblock - 1)

    for b in range(nb):
        o_ref[b] = jnp.concatenate(
            [otb_ref[s, seq_rows(b), :] for s in range(nslab)], axis=1).astype(BF16)


def _s5(u3, bm, cm, coef, d, wg, bg):
    nb, seq, width = u3.shape
    tt = S5_TIME
    ncols = coef.shape[1]
    kern = functools.partial(_s5_kernel, nb=nb, tt=tt, width=width)
    return pl.pallas_call(
        kern,
        grid=(seq // tt,),
        in_specs=[pl.BlockSpec((nb, tt, width), lambda i: (0, i, 0)),
                  _const_spec(bm.shape), _const_spec(cm.shape), _const_spec(coef.shape),
                  _const_spec(d.shape), _const_spec(wg.shape), _const_spec(bg.shape)],
        out_specs=pl.BlockSpec((nb, tt, width), lambda i: (0, i, 0)),
        out_shape=jax.ShapeDtypeStruct((nb, seq, width), BF16),
        scratch_shapes=[pltpu.VMEM((width // LANES, tt * nb, LANES), F32),
                        pltpu.VMEM((tt * nb, ncols), F32),
                        pltpu.VMEM((tt * nb, ncols), F32),
                        pltpu.VMEM((nb, ncols), F32),
                        pltpu.VMEM((width // LANES, tt * nb, LANES), F32)],
        compiler_params=_params("arbitrary"),
        name="s5_mixer",
    )(u3, bm, cm, coef, d, wg, bg)


def _ffn_tail(x, w1_ref, b1_ref, w2_ref, b2_ref, g_ref, be_ref):
    xb = x.astype(BF16)
    acc = ALPHA * x + b2_ref[...]
    for c in range(w1_ref.shape[1] // FFN_CHUNK):
        cs = slice(c * FFN_CHUNK, (c + 1) * FFN_CHUNK)
        h = jnp.dot(xb, w1_ref[:, cs], preferred_element_type=F32) + b1_ref[:, cs]
        h = jnp.square(jnp.maximum(h, 0.0))
        acc = acc + jnp.dot(h.astype(BF16), w2_ref[cs, :], preferred_element_type=F32)
    return _layer_norm(acc, g_ref[...], be_ref[...])


def _ffn_kernel(x_ref, w1_ref, b1_ref, w2_ref, b2_ref, g_ref, be_ref, o_ref):
    o_ref[...] = _ffn_tail(x_ref[...], w1_ref, b1_ref, w2_ref, b2_ref, g_ref, be_ref)


def _mix_ffn_kernel(x_ref, a_ref, s_ref, wo_ref, bo_ref, g1_ref, be1_ref,
                    w1_ref, b1_ref, w2_ref, b2_ref, g2_ref, be2_ref, o_ref, *, s5w):
    mix = (jnp.dot(a_ref[...], wo_ref[:s5w, :], preferred_element_type=F32)
           + jnp.dot(s_ref[...], wo_ref[s5w:, :], preferred_element_type=F32) + bo_ref[...])
    x1 = _layer_norm(ALPHA * x_ref[...] + mix, g1_ref[...], be1_ref[...])
    o_ref[...] = _ffn_tail(x1, w1_ref, b1_ref, w2_ref, b2_ref, g2_ref, be2_ref)


def _row_spec(width):
    return pl.BlockSpec((FFN_ROWS, width), lambda i: (i, 0))


def _ffn(x2, w1, b1, w2, b2, g, be):
    n, d = x2.shape
    consts = (w1, b1, w2, b2, g, be)
    return pl.pallas_call(
        _ffn_kernel,
        grid=(n // FFN_ROWS,),
        in_specs=[_row_spec(d)] + [_const_spec(c.shape) for c in consts],
        out_specs=_row_spec(d),
        out_shape=jax.ShapeDtypeStruct((n, d), F32),
        compiler_params=_params("parallel"),
        name="ffn_ln",
    )(x2, *consts)


def _mix_ffn(x2, s5o, sbo, wo, bo, g1, be1, w1, b1, w2, b2, g2, be2):
    n, d = x2.shape
    s5w, sbw = s5o.shape[1], sbo.shape[1]
    consts = (wo, bo, g1, be1, w1, b1, w2, b2, g2, be2)
    return pl.pallas_call(
        functools.partial(_mix_ffn_kernel, s5w=s5w),
        grid=(n // FFN_ROWS,),
        in_specs=[_row_spec(d), _row_spec(s5w), _row_spec(sbw)]
                 + [_const_spec(c.shape) for c in consts],
        out_specs=_row_spec(d),
        out_shape=jax.ShapeDtypeStruct((n, d), F32),
        compiler_params=_params("parallel"),
        name="out_proj_ffn_ln",
    )(x2, s5o, sbo, *consts)


def _conv_kernel(x_ref, w1_ref, b1_ref, wd_ref, bd_ref, cg_ref, cb_ref, w2_ref, b2_ref,
                 g_ref, be_ref, o_ref, buf_ref, cv_ref, *, width):
    t = CONV_TILE
    nslab = width // LANES

    @pl.when(pl.program_id(1) == 0)
    def _():
        buf_ref[:, 0:CONV_HIST, :] = jnp.zeros((nslab, CONV_HIST, LANES), F32)

    x = x_ref[...]
    h = _bdot(x, w1_ref[...]) + b1_ref[...]
    glu = h[:, :width] * _sigmoid(h[:, width:])
    for cb in range(nslab):
        buf_ref[cb, CONV_HIST:CONV_HIST + t, :] = glu[:, cb * LANES:(cb + 1) * LANES]

    first = CONV_HIST - (CONV_SIZE - 1)
    for cb in range(nslab):
        cs = slice(cb * LANES, (cb + 1) * LANES)
        for par in range(2):
            acc = jnp.broadcast_to(bd_ref[:, cs], (t // 2, LANES))
            for k in range(CONV_SIZE):
                acc = acc + wd_ref[k:k + 1, cs] * buf_ref[cb, pl.ds(first + k + par, t // 2, stride=2), :]
            cv_ref[cb, pl.ds(par, t // 2, stride=2), :] = acc
        buf_ref[cb, 0:CONV_HIST, :] = buf_ref[cb, t:t + CONV_HIST, :]

    conv = jnp.concatenate([cv_ref[cb] for cb in range(nslab)], axis=1)
    c = _layer_norm(conv, cg_ref[...], cb_ref[...])
    c = c * _sigmoid(c)
    mix = _bdot(c, w2_ref[...]) + b2_ref[...]
    o_ref[...] = _layer_norm(ALPHA * x + mix, g_ref[...], be_ref[...])


def _conv_layer(x3, w1, b1, wd, bd, cg, cb, w2, b2, g, be):
    bsz, seq, d = x3.shape
    width = w2.shape[0]
    t = CONV_TILE
    blk = pl.BlockSpec((None, t, d), lambda b, i: (b, i, 0))
    consts = (w1, b1, wd, bd, cg, cb, w2, b2, g, be)
    return pl.pallas_call(
        functools.partial(_conv_kernel, width=width),
        grid=(bsz, seq // t),
        in_specs=[blk] + [_const_spec(c.shape) for c in consts],
        out_specs=blk,
        out_shape=jax.ShapeDtypeStruct((bsz, seq, d), F32),
        scratch_shapes=[pltpu.VMEM((width // LANES, CONV_HIST + t, LANES), F32),
                        pltpu.VMEM((width // LANES, t, LANES), F32)],
        compiler_params=_params("parallel", "arbitrary"),
        name="conformer_conv_ln",
    )(x3, *consts)


def _row(v):
    return v.astype(F32).reshape(1, -1)


def kernel(x, ln1_g, ln1_b, ln2_g, ln2_b, ffn_w1, ffn_b1, ffn_w2, ffn_b2, mix_w_in, mix_b_in, s5_lambda_re, s5_lambda_im, s5_log_dt, s5_b_re, s5_b_im, s5_c_re, s5_c_im, s5_d, s5_w_glu, s5_b_glu, mix_w_out, mix_b_out, conv_w_pw1, conv_b_pw1, conv_w_dw, conv_b_dw, conv_ln_g, conv_ln_b, conv_w_pw2, conv_b_pw2):
    bsz, seq, d = x.shape
    depth = ln1_g.shape[0]
    s5w = s5_d.shape[1]
    sbw = (mix_w_in.shape[2] - s5w) // 3
    n = bsz * seq
    x2 = x.reshape(n, d)
    for layer in range(depth):
        i = layer // 2
        ffn = (ffn_w1[layer].astype(BF16), _row(ffn_b1[layer]),
               ffn_w2[layer].astype(BF16), _row(ffn_b2[layer]),
               _row(ln2_g[layer]), _row(ln2_b[layer]))
        if layer % 2 == 0:
            u, qkv = _in_proj(x2, mix_w_in[i].astype(BF16), _row(mix_b_in[i]), s5w, sbw)
            sbo = _attention(qkv.reshape(bsz, seq, 3 * sbw), sbw)
            bm, cm, coef = _s5_tables(s5_lambda_re[i], s5_lambda_im[i], s5_log_dt[i],
                                      s5_b_re[i], s5_b_im[i], s5_c_re[i], s5_c_im[i])
            s5o = _s5(u.reshape(bsz, seq, s5w), bm, cm, coef, _row(s5_d[i]),
                      s5_w_glu[i].astype(BF16), _row(s5_b_glu[i]))
            x2 = _mix_ffn(x2, s5o.reshape(n, s5w), sbo.reshape(n, sbw),
                          mix_w_out[i].astype(BF16), _row(mix_b_out[i]),
                          _row(ln1_g[layer]), _row(ln1_b[layer]), *ffn)
        else:
            x1 = _conv_layer(x2.reshape(bsz, seq, d), conv_w_pw1[i].astype(BF16),
                             _row(conv_b_pw1[i]), conv_w_dw[i].astype(F32), _row(conv_b_dw[i]),
                             _row(conv_ln_g[i]), _row(conv_ln_b[i]),
                             conv_w_pw2[i].astype(BF16), _row(conv_b_pw2[i]),
                             _row(ln1_g[layer]), _row(ln1_b[layer]))
            x2 = _ffn(x1.reshape(n, d), *ffn)
    return x2.reshape(bsz, seq, d)
```
